```python
import math
import jax, jax.numpy as jnp
from jax import lax
import numpy as np

D_MODEL = 2048
BATCH = 4
SEQ = 8192
DEPTH = 1

N_META = 16
BLOCK = 128
PAD = BLOCK - N_META
HG_HEADS = 8
HG_DK = 128
HG_DV = 128
HG_CHUNK = 64
SB_HEADS = 8
SB_DH = 128
D_FF = 5632
LN_EPS = 1e-5
RMS_EPS = 1e-6
DN_ALPHA = (2.0 * DEPTH) ** 0.25
DN_BETA = (8.0 * DEPTH) ** -0.25

HG_QK_W = HG_HEADS * HG_DK
HG_V_W = HG_HEADS * HG_DV
SB_W = SB_HEADS * SB_DH
SPLIT_IDX = (HG_QK_W, 2 * HG_QK_W, 2 * HG_QK_W + HG_V_W, 2 * HG_QK_W + 2 * HG_V_W,
             2 * HG_QK_W + 2 * HG_V_W + SB_W, 2 * HG_QK_W + 2 * HG_V_W + 2 * SB_W,
             2 * HG_QK_W + 2 * HG_V_W + 3 * SB_W)
IN_COLS = SPLIT_IDX[-1] + 2 * D_MODEL

kernel_name = "hybrid_hgrn2_stickbreaking_macaron_deepnorm"


def layer_norm(x, g, b):
    xf = x.astype(jnp.float32)
    mu = jnp.mean(xf, axis=-1, keepdims=True)
    var = jnp.mean(jnp.square(xf - mu), axis=-1, keepdims=True)
    return ((xf - mu) * lax.rsqrt(var + LN_EPS) * g.astype(jnp.float32) + b.astype(jnp.float32)).astype(x.dtype)


def swiglu(x, w_gate, w_up, w_down):
    return (jax.nn.silu(x @ w_gate) * (x @ w_up)) @ w_down


def hgrn2_chunked(q, k, v, log_f):
    B, L, H, DK = q.shape
    DV = v.shape[-1]
    n = L // HG_CHUNK

    def to_chunks(t):
        return t.astype(jnp.float32).reshape(B, n, HG_CHUNK, H, t.shape[-1]).transpose(1, 0, 3, 2, 4)

    qc, kc, vc, gc = to_chunks(q), to_chunks(k), to_chunks(v), to_chunks(log_f)
    causal = jnp.tril(jnp.ones((HG_CHUNK, HG_CHUNK), dtype=bool))[None, None, :, :, None]

    def step(S, inp):
        qi, ki, vi, gi = inp
        b = jnp.cumsum(gi, axis=2)
        o_inter = jnp.einsum('bhtk,bhkv->bhtv', qi * jnp.exp(b), S)
        rel = jnp.where(causal, b[:, :, :, None, :] - b[:, :, None, :, :], -jnp.inf)
        scores = jnp.einsum('bhtk,bhsk,bhtsk->bhts', qi, ki, jnp.exp(rel))
        o_intra = jnp.einsum('bhts,bhsv->bhtv', scores, vi)
        b_last = b[:, :, -1:, :]
        S_new = jnp.exp(b_last[:, :, 0, :])[..., None] * S + jnp.einsum(
            'bhsk,bhsv->bhkv', ki * jnp.exp(b_last - b), vi)
        return S_new, o_inter + o_intra

    S0 = jnp.zeros((B, H, DK, DV), jnp.float32)
    _, o = lax.scan(step, S0, (qc, kc, vc, gc))
    return o.transpose(1, 0, 3, 2, 4).reshape(B, L, H, DV)


def stick_breaking(q, k, v, key_valid):
    B, L, H, D = q.shape
    nb = L // BLOCK
    scale = 1.0 / math.sqrt(D)
    qb = q.astype(jnp.float32).reshape(B, nb, BLOCK, H, D).transpose(1, 0, 3, 2, 4)
    kf = k.astype(jnp.float32)
    vf = v.astype(jnp.float32)
    kpos = jnp.arange(L)

    def one_block(args):
        qi, start = args
        z = jnp.einsum('bhtd,bshd->bhts', qi, kf) * scale
        qpos = start + jnp.arange(BLOCK)
        mask = (kpos[None, :] < qpos[:, None]) & key_valid[None, :]
        log_beta = jax.nn.log_sigmoid(z)
        log_1mb = jnp.where(mask, jax.nn.log_sigmoid(-z), 0.0)
        suffix = jnp.flip(jnp.cumsum(jnp.flip(log_1mb, axis=-1), axis=-1), axis=-1)
        w = jnp.where(mask, jnp.exp(log_beta + suffix - log_1mb), 0.0)
        return jnp.einsum('bhts,bshd->bhtd', w, vf)

    starts = jnp.arange(nb) * BLOCK
    o = lax.map(one_block, (qb, starts))
    return o.transpose(1, 0, 3, 2, 4).reshape(B, L, H, D)


def gated_mixer(h, valid, w_in, b_gate, lb, hg_norm_g, w_proj_hg, w_proj_sb, w_out):
    B, L, _ = h.shape
    proj = h @ w_in
    hq, hf, hi, hog, sq, sk, sv, gates = jnp.split(proj, SPLIT_IDX, axis=-1)
    vmask = valid[:, None]

    f = lb + (1.0 - lb) * jax.nn.sigmoid(hf.astype(jnp.float32))
    log_f = jnp.where(vmask, jnp.log(f), 0.0)
    k_hg = jnp.where(vmask, 1.0 - f, 0.0)
    q_hg = jax.nn.silu(hq.astype(jnp.float32))
    o_hg = hgrn2_chunked(q_hg.reshape(B, L, HG_HEADS, HG_DK), k_hg.reshape(B, L, HG_HEADS, HG_DK),
                         hi.reshape(B, L, HG_HEADS, HG_DV), log_f.reshape(B, L, HG_HEADS, HG_DK))
    o_hg = o_hg * lax.rsqrt(jnp.mean(jnp.square(o_hg), axis=-1, keepdims=True) + RMS_EPS)
    o_hg = o_hg * hg_norm_g.astype(jnp.float32).reshape(HG_HEADS, HG_DV)
    o_hg = (o_hg.reshape(B, L, HG_V_W) * jax.nn.silu(hog.astype(jnp.float32))).astype(h.dtype)

    o_sb = stick_breaking(sq.reshape(B, L, SB_HEADS, SB_DH), sk.reshape(B, L, SB_HEADS, SB_DH),
                          sv.reshape(B, L, SB_HEADS, SB_DH), valid)
    o_sb = o_sb.reshape(B, L, SB_W).astype(h.dtype)

    g = jax.nn.sigmoid((gates + b_gate).astype(jnp.float32)).astype(h.dtype)
    g_hg, g_sb = jnp.split(g, 2, axis=-1)
    y = g_hg * (o_hg @ w_proj_hg) + g_sb * (o_sb @ w_proj_sb)
    return y @ w_out


def setup_inputs(seed: int = 0) -> dict:
    key = jax.random.key(seed)
    ks = jax.random.split(key, 24)
    f32 = jnp.float32

    def nrm(k, shape, scale):
        return jax.random.normal(k, shape, f32) * scale

    def gain(k, shape):
        return 1.0 + 0.02 * jax.random.normal(k, shape, f32)

    d_inv = D_MODEL ** -0.5
    return {
        "x": nrm(ks[0], (BATCH, SEQ, D_MODEL), 1.0),
        "meta": nrm(ks[1], (N_META, D_MODEL), 1.0),
        "ln1_g": gain(ks[2], (DEPTH, D_MODEL)),
        "ln1_b": nrm(ks[3], (DEPTH, D_MODEL), 0.02),
        "ffn1_w_gate": nrm(ks[4], (DEPTH, D_MODEL, D_FF), d_inv),
        "ffn1_w_up": nrm(ks[5], (DEPTH, D_MODEL, D_FF), d_inv),
        "ffn1_w_down": nrm(ks[6], (DEPTH, D_FF, D_MODEL), D_FF ** -0.5 * DN_BETA),
        "w_in": nrm(ks[7], (DEPTH, D_MODEL, IN_COLS), d_inv),
        "b_gate": nrm(ks[8], (DEPTH, 2 * D_MODEL), 0.1),
        "hg_lb_logits": nrm(ks[9], (DEPTH + 1, HG_QK_W), 0.1),
        "hg_norm_g": gain(ks[10], (DEPTH, HG_V_W)),
        "w_proj_hg": nrm(ks[11], (DEPTH, HG_V_W, D_MODEL), HG_V_W ** -0.5),
        "w_proj_sb": nrm(ks[12], (DEPTH, SB_W, D_MODEL), SB_W ** -0.5),
        "w_out": nrm(ks[13], (DEPTH, D_MODEL, D_MODEL), d_inv * DN_BETA),
        "ln2_g": gain(ks[14], (DEPTH, D_MODEL)),
        "ln2_b": nrm(ks[15], (DEPTH, D_MODEL), 0.02),
        "ffn2_w_gate": nrm(ks[16], (DEPTH, D_MODEL, D_FF), d_inv),
        "ffn2_w_up": nrm(ks[17], (DEPTH, D_MODEL, D_FF), d_inv),
        "ffn2_w_down": nrm(ks[18], (DEPTH, D_FF, D_MODEL), D_FF ** -0.5 * DN_BETA),
        "ln3_g": gain(ks[19], (DEPTH, D_MODEL)),
        "ln3_b": nrm(ks[20], (DEPTH, D_MODEL), 0.02),
    }


def reference(x, meta, ln1_g, ln1_b, ffn1_w_gate, ffn1_w_up, ffn1_w_down, w_in, b_gate, hg_lb_logits,
              hg_norm_g, w_proj_hg, w_proj_sb, w_out, ln2_g, ln2_b, ffn2_w_gate, ffn2_w_up, ffn2_w_down,
              ln3_g, ln3_b):
    B, S, D = x.shape
    pad = jnp.zeros((B, PAD, D), x.dtype)
    meta_b = jnp.broadcast_to(meta.astype(x.dtype)[None], (B, N_META, D))
    h = jnp.concatenate([pad, meta_b, x], axis=1)
    L = h.shape[1]
    valid = jnp.arange(L) >= PAD

    lb_all = jnp.cumsum(jax.nn.softmax(hg_lb_logits.astype(jnp.float32), axis=0), axis=0)

    for l in range(DEPTH):
        h = layer_norm(DN_ALPHA * h + 0.5 * swiglu(h, ffn1_w_gate[l], ffn1_w_up[l], ffn1_w_down[l]),
                       ln1_g[l], ln1_b[l])
        mix = gated_mixer(h, valid, w_in[l], b_gate[l], lb_all[l], hg_norm_g[l],
                          w_proj_hg[l], w_proj_sb[l], w_out[l])
        h = layer_norm(DN_ALPHA * h + mix, ln2_g[l], ln2_b[l])
        h = layer_norm(DN_ALPHA * h + 0.5 * swiglu(h, ffn2_w_gate[l], ffn2_w_up[l], ffn2_w_down[l]),
                       ln3_g[l], ln3_b[l])

    return h[:, PAD + N_META:]
```

```python
import functools
import math

import jax
import jax.numpy as jnp
from jax import lax
from jax.experimental import pallas as pl
from jax.experimental.pallas import tpu as pltpu

N_META = 16
BLOCK = 128
PAD = BLOCK - N_META
HEADS = 8
HEAD_DIM = 128
MIX_W = HEADS * HEAD_DIM
SUB = 16
LN_EPS = 1e-5
RMS_EPS = 1e-6
DEPTH = 1
DN_ALPHA = (2.0 * DEPTH) ** 0.25
SB_DEAD_LOG = -104.0
VMEM_LIMIT = 56 * 1024 * 1024

F32 = jnp.float32
BF16 = jnp.bfloat16


def _layer_norm(y, g, b):
    mu = jnp.mean(y, axis=-1, keepdims=True)
    yc = y - mu
    var = jnp.mean(yc * yc, axis=-1, keepdims=True)
    return yc * lax.rsqrt(var + LN_EPS) * g + b


def _dot(a, b):
    return jnp.dot(a, b, preferred_element_type=F32)


def _dot_nt(a, b):
    return lax.dot_general(a, b, (((1,), (1,)), ((), ())), preferred_element_type=F32)


def _dot_tn(a, b):
    return lax.dot_general(a, b, (((0,), (0,)), ((), ())), preferred_element_type=F32)


def _ffn_ln_kernel(x_ref, wg_ref, wu_ref, wd_ref, g_ref, b_ref, o_ref, xb_ref, acc_ref):
    j = pl.program_id(1)

    @pl.when(j == 0)
    def _():
        xb_ref[...] = x_ref[...].astype(BF16)
        acc_ref[...] = jnp.zeros_like(acc_ref)

    xb = xb_ref[...]
    gate = _dot(xb, wg_ref[...])
    up = _dot(xb, wu_ref[...])
    act = (gate * jax.nn.sigmoid(gate) * up).astype(BF16)
    acc_ref[...] += _dot(act, wd_ref[...])

    @pl.when(j == pl.num_programs(1) - 1)
    def _():
        y = DN_ALPHA * x_ref[...] + 0.5 * acc_ref[...]
        o_ref[...] = _layer_norm(y, g_ref[...], b_ref[...])


def _ffn_ln(x, wg, wu, wd, g, b, *, tm, tf):
    n, d = x.shape
    dff = wg.shape[1]
    assert n % tm == 0 and dff % tf == 0
    return pl.pallas_call(
        _ffn_ln_kernel,
        grid=(n // tm, dff // tf),
        in_specs=[
            pl.BlockSpec((tm, d), lambda i, j: (i, 0)),
            pl.BlockSpec((d, tf), lambda i, j: (0, j)),
            pl.BlockSpec((d, tf), lambda i, j: (0, j)),
            pl.BlockSpec((tf, d), lambda i, j: (j, 0)),
            pl.BlockSpec((1, d), lambda i, j: (0, 0)),
            pl.BlockSpec((1, d), lambda i, j: (0, 0)),
        ],
        out_specs=pl.BlockSpec((tm, d), lambda i, j: (i, 0)),
        out_shape=jax.ShapeDtypeStruct((n, d), F32),
        scratch_shapes=[pltpu.VMEM((tm, d), BF16), pltpu.VMEM((tm, d), F32)],
        compiler_params=pltpu.CompilerParams(
            dimension_semantics=("parallel", "arbitrary"), vmem_limit_bytes=VMEM_LIMIT),
        name="ffn_ln",
    )(x, wg, wu, wd, g, b)


N_SEG = 7


def _proj_kernel(x_ref, w_ref, bg_ref, lbl_ref, q_ref, lf_ref, v_ref, og_ref, sq_ref, sk_ref, sv_ref,
                 gt_ref, xb_ref, *, n_inert, tm):
    i = pl.program_id(0)
    j = pl.program_id(1)

    @pl.when(j == 0)
    def _():
        xb_ref[...] = x_ref[...].astype(BF16)

    acc = _dot(xb_ref[...], w_ref[...])

    @pl.when(j == 0)
    def _():
        q_ref[...] = (acc * jax.nn.sigmoid(acc)).astype(BF16)

    @pl.when(j == 1)
    def _():
        l0 = lbl_ref[0:1, :]
        l1 = lbl_ref[1:2, :]
        m = jnp.maximum(l0, l1)
        e0 = jnp.exp(l0 - m)
        e1 = jnp.exp(l1 - m)
        lb = e0 / (e0 + e1)
        f = lb + (1.0 - lb) * jax.nn.sigmoid(acc)
        log_f = jnp.log(f)
        if n_inert:
            row = i * tm + lax.broadcasted_iota(jnp.int32, log_f.shape, 0)
            log_f = jnp.where(row >= n_inert, log_f, 0.0)
        lf_ref[...] = log_f

    @pl.when(j == 2)
    def _():
        v_ref[...] = acc.astype(BF16)

    @pl.when(j == 3)
    def _():
        og_ref[...] = (acc * jax.nn.sigmoid(acc)).astype(BF16)

    @pl.when(j == 4)
    def _():
        sq_ref[...] = (acc * (1.0 / math.sqrt(HEAD_DIM))).astype(BF16)

    @pl.when(j == 5)
    def _():
        sk_ref[...] = acc.astype(BF16)

    @pl.when(j == 6)
    def _():
        sv_ref[...] = acc.astype(BF16)

    @pl.when(j >= N_SEG)
    def _():
        gt_ref[...] = jax.nn.sigmoid(acc + bg_ref[...]).astype(BF16)


def _proj(x, w_in, b_gate, lb_logits, *, tm, n_inert):
    n, d = x.shape
    cols = w_in.shape[1]
    tn = MIX_W
    assert n % tm == 0 and cols == N_SEG * MIX_W + 2 * d and (2 * d) % tn == 0
    seg = pl.BlockSpec((tm, tn), lambda i, j: (i, 0))
    gate_idx = lambda i, j: (i, jnp.maximum(j - N_SEG, 0))
    out_shape = [jax.ShapeDtypeStruct((n, MIX_W), BF16) for _ in range(N_SEG)]
    out_shape[1] = jax.ShapeDtypeStruct((n, MIX_W), F32)
    out_shape.append(jax.ShapeDtypeStruct((n, 2 * d), BF16))
    return pl.pallas_call(
        functools.partial(_proj_kernel, n_inert=n_inert, tm=tm),
        grid=(n // tm, cols // tn),
        in_specs=[
            pl.BlockSpec((tm, d), lambda i, j: (i, 0)),
            pl.BlockSpec((d, tn), lambda i, j: (0, j)),
            pl.BlockSpec((1, tn), lambda i, j: (0, jnp.maximum(j - N_SEG, 0))),
            pl.BlockSpec((DEPTH + 1, MIX_W), lambda i, j: (0, 0)),
        ],
        out_specs=[seg] * N_SEG + [pl.BlockSpec((tm, tn), gate_idx)],
        out_shape=out_shape,
        scratch_shapes=[pltpu.VMEM((tm, d), BF16)],
        compiler_params=pltpu.CompilerParams(
            dimension_semantics=("parallel", "arbitrary"), vmem_limit_bytes=VMEM_LIMIT),
        name="mixer_proj",
    )(x, w_in, b_gate, lb_logits)


def _hgrn_kernel(q_ref, lf_ref, v_ref, og_ref, gain_ref, st0_ref, o_ref, stf_ref, st_ref, fpad_ref,
                 vpad_ref, *, tt, chunk):
    t_id = pl.program_id(2)

    @pl.when(t_id == 0)
    def _():
        st_ref[...] = st0_ref[...]

    q = q_ref[...].astype(F32)
    g = lf_ref[...]
    vb = v_ref[...]
    v = vb.astype(F32)
    f = jnp.exp(g)
    kk = 1.0 - f

    row = lax.broadcasted_iota(jnp.int32, (tt, HEAD_DIM), 0)
    r_chunk = row & (chunk - 1)
    r_sub = row & (SUB - 1)

    b = g
    s = 1
    while s < chunk:
        b = b + jnp.where(r_chunk >= s, pltpu.roll(b, s, axis=0), 0.0)
        s *= 2

    fpad_ref[0:SUB, :] = jnp.zeros((SUB, HEAD_DIM), F32)
    vpad_ref[0:SUB, :] = jnp.zeros((SUB, HEAD_DIM), F32)
    fpad_ref[SUB:, :] = f
    vpad_ref[SUB:, :] = v
    o_band = jnp.sum(q * kk, axis=-1, keepdims=True) * v
    decay = jnp.ones((tt, HEAD_DIM), F32)
    for d in range(1, SUB):
        decay = jnp.where(r_sub >= d, decay * fpad_ref[pl.ds(SUB - d + 1, tt), :], 0.0)
        f_d = fpad_ref[pl.ds(SUB - d, tt), :]
        v_d = vpad_ref[pl.ds(SUB - d, tt), :]
        o_band = o_band + jnp.sum(q * (1.0 - f_d) * decay, axis=-1, keepdims=True) * v_d

    ends = b.reshape(tt // SUB, SUB, HEAD_DIM)[:, SUB - 1:SUB, :]
    e_next = jnp.broadcast_to(ends, (tt // SUB, SUB, HEAD_DIM)).reshape(tt, HEAD_DIM)
    e_own = jnp.where(r_chunk >= SUB, pltpu.roll(e_next, SUB, axis=0), 0.0)

    q_sub = q * jnp.exp(b - e_own)
    k_sub = (kk * jnp.exp(e_next - b)).astype(BF16)
    n_gap = chunk // SUB - 1
    q_gap = [q_sub.astype(BF16)]
    for gap in range(1, n_gap):
        ok = r_chunk >= SUB * (gap + 1)
        e_far = pltpu.roll(e_own, SUB * gap, axis=0)
        q_gap.append((q_sub * jnp.exp(jnp.where(ok, e_own - e_far, 0.0))).astype(BF16))
    q_state = (q * jnp.exp(b)).astype(BF16)

    ci = lax.broadcasted_iota(jnp.int32, (chunk, chunk), 0) // SUB
    cj = lax.broadcasted_iota(jnp.int32, (chunk, chunk), 1) // SUB
    sub_dist = ci - cj

    st = st_ref[...]
    outs = []
    for c in range(tt // chunk):
        lo, hi = c * chunk, (c + 1) * chunk
        scores = jnp.zeros((chunk, chunk), F32)
        for gap in range(n_gap):
            scores = scores + jnp.where(sub_dist == gap + 1, _dot_nt(q_gap[gap][lo:hi], k_sub[lo:hi]), 0.0)
        o_c = o_band[lo:hi] + _dot(scores.astype(BF16), vb[lo:hi]) + _dot_nt(q_state[lo:hi], st.astype(BF16))
        b_end = b[hi - 1:hi, :]
        k_end = (kk[lo:hi] * jnp.exp(b_end - b[lo:hi])).astype(BF16)
        st = st * jnp.exp(b_end) + _dot_tn(vb[lo:hi], k_end)
        outs.append(o_c)
    st_ref[...] = st
    o = jnp.concatenate(outs, axis=0) if len(outs) > 1 else outs[0]

    o = o * lax.rsqrt(jnp.mean(o * o, axis=-1, keepdims=True) + RMS_EPS)
    o_ref[...] = (o * gain_ref[...] * og_ref[...].astype(F32)).astype(BF16)

    @pl.when(t_id == pl.num_programs(2) - 1)
    def _():
        stf_ref[...] = st


def _hgrn(q, lf, v, og, gain, st0, *, tt, chunk):
    bsz, length, _ = q.shape
    assert length % tt == 0 and tt % chunk == 0 and chunk % SUB == 0
    tok = pl.BlockSpec((None, tt, HEAD_DIM), lambda b, h, t: (b, t, h))
    return pl.pallas_call(
        functools.partial(_hgrn_kernel, tt=tt, chunk=chunk),
        grid=(bsz, HEADS, length // tt),
        in_specs=[tok, tok, tok, tok,
                  pl.BlockSpec((1, HEAD_DIM), lambda b, h, t: (0, h)),
                  pl.BlockSpec((None, HEAD_DIM, HEAD_DIM), lambda b, h, t: (h, 0, 0))],
        out_specs=[tok, pl.BlockSpec((None, None, HEAD_DIM, HEAD_DIM), lambda b, h, t: (b, h, 0, 0))],
        out_shape=[jax.ShapeDtypeStruct((bsz, length, MIX_W), BF16),
                   jax.ShapeDtypeStruct((bsz, HEADS, HEAD_DIM, HEAD_DIM), F32)],
        scratch_shapes=[pltpu.VMEM((HEAD_DIM, HEAD_DIM), F32),
                        pltpu.VMEM((tt + SUB, HEAD_DIM), F32),
                        pltpu.VMEM((tt + SUB, HEAD_DIM), F32)],
        compiler_params=pltpu.CompilerParams(
            dimension_semantics=("parallel", "parallel", "arbitrary"), vmem_limit_bytes=VMEM_LIMIT),
        name="hgrn2",
    )(q, lf, v, og, gain, st0)


def _sb_kernel(q_ref, k_ref, v_ref, kp_ref, vp_ref, tri_ref, o_ref, acc_ref, rem_ref, *, tq):
    qi = pl.program_id(2)
    q = q_ref[...]
    acc_ref[...] = jnp.zeros_like(acc_ref)
    rem_ref[...] = jnp.zeros_like(rem_ref)

    def visit(k_t, v_t, mask):
        z = _dot_nt(q, k_t)
        softplus = jnp.log(1.0 + jnp.exp(-jnp.abs(z)))
        log_beta = jnp.minimum(z, 0.0) - softplus
        log_rest = -jnp.maximum(z, 0.0) - softplus
        if mask is not None:
            log_rest = jnp.where(mask, log_rest, 0.0)
        hi = log_rest.astype(BF16)
        lo = (log_rest - hi.astype(F32)).astype(BF16)
        later = _dot(jnp.concatenate([hi, lo], axis=1), tri_ref[...])
        rem = rem_ref[...]
        w = jnp.exp(log_beta + later + rem)
        if mask is not None:
            w = jnp.where(mask, w, 0.0)
        acc_ref[...] += _dot(w.astype(BF16), v_t)
        rem = rem + jnp.sum(log_rest, axis=-1, keepdims=True)
        rem_ref[...] = rem
        return jnp.max(rem)

    r_i = lax.broadcasted_iota(jnp.int32, (tq, tq), 0)
    c_i = lax.broadcasted_iota(jnp.int32, (tq, tq), 1)
    start = pl.multiple_of(qi * tq, tq)
    live = visit(k_ref[pl.ds(start, tq), :], v_ref[pl.ds(start, tq), :], c_i < r_i)

    def cond(state):
        j, live = state
        return jnp.logical_and(j >= 0, live > SB_DEAD_LOG)

    def body(state):
        j, _ = state
        s0 = pl.multiple_of(j * tq, tq)
        return j - 1, visit(k_ref[pl.ds(s0, tq), :], v_ref[pl.ds(s0, tq), :], None)

    _, live = lax.while_loop(cond, body, (qi - 1, live))

    @pl.when(live > SB_DEAD_LOG)
    def _():
        visit(kp_ref[...], vp_ref[...], jnp.logical_and(c_i >= PAD, c_i < BLOCK))

    o_ref[...] = acc_ref[...].astype(BF16)


def _stick_breaking(q, k, v, kp, vp, *, tq):
    bsz, length, _ = q.shape
    assert length % tq == 0 and tq >= BLOCK
    r = lax.broadcasted_iota(jnp.int32, (2 * tq, tq), 0) % tq
    c = lax.broadcasted_iota(jnp.int32, (2 * tq, tq), 1)
    tri = (r > c).astype(BF16)
    seq = pl.BlockSpec((None, length, HEAD_DIM), lambda b, h, i: (b, 0, h))
    tile = pl.BlockSpec((None, tq, HEAD_DIM), lambda b, h, i: (b, i, h))
    pre = pl.BlockSpec((tq, HEAD_DIM), lambda b, h, i: (0, h))
    return pl.pallas_call(
        functools.partial(_sb_kernel, tq=tq),
        grid=(bsz, HEADS, length // tq),
        in_specs=[tile, seq, seq, pre, pre, pl.BlockSpec((2 * tq, tq), lambda b, h, i: (0, 0))],
        out_specs=tile,
        out_shape=jax.ShapeDtypeStruct((bsz, length, MIX_W), BF16),
        scratch_shapes=[pltpu.VMEM((tq, HEAD_DIM), F32), pltpu.VMEM((tq, 1), F32)],
        compiler_params=pltpu.CompilerParams(
            dimension_semantics=("parallel", "parallel", "arbitrary"), vmem_limit_bytes=VMEM_LIMIT),
        name="stick_breaking",
    )(q, k, v, kp, vp, tri)


def _merge_ln_kernel(h_ref, ohg_ref, osb_ref, ghg_ref, gsb_ref, php_ref, psb_ref, wo_ref, g_ref, b_ref, o_ref):
    y = (ghg_ref[...].astype(F32) * _dot(ohg_ref[...], php_ref[...])
         + gsb_ref[...].astype(F32) * _dot(osb_ref[...], psb_ref[...]))
    mix = _dot(y.astype(BF16), wo_ref[...])
    o_ref[...] = _layer_norm(DN_ALPHA * h_ref[...] + mix, g_ref[...], b_ref[...])


def _merge_ln(h, o_hg, o_sb, gates, p_hg, p_sb, w_out, g, b, *, tm):
    n, d = h.shape
    assert n % tm == 0
    row = lambda w: pl.BlockSpec((tm, w), lambda i: (i, 0))
    full = lambda a: pl.BlockSpec(a.shape, lambda i: (0, 0))
    return pl.pallas_call(
        _merge_ln_kernel,
        grid=(n // tm,),
        in_specs=[row(d), row(MIX_W), row(MIX_W),
                  pl.BlockSpec((tm, d), lambda i: (i, 0)), pl.BlockSpec((tm, d), lambda i: (i, 1)),
                  full(p_hg), full(p_sb), full(w_out), full(g), full(b)],
        out_specs=row(d),
        out_shape=jax.ShapeDtypeStruct((n, d), F32),
        compiler_params=pltpu.CompilerParams(
            dimension_semantics=("parallel",), vmem_limit_bytes=VMEM_LIMIT),
        name="merge_ln",
    )(h, o_hg, o_sb, gates, gates, p_hg, p_sb, w_out, g, b)


def _pick(n, pref):
    t = min(n, pref)
    while n % t:
        t //= 2
    return t


def kernel(x, meta, ln1_g, ln1_b, ffn1_w_gate, ffn1_w_up, ffn1_w_down, w_in, b_gate, hg_lb_logits, hg_norm_g, w_proj_hg, w_proj_sb, w_out, ln2_g, ln2_b, ffn2_w_gate, ffn2_w_up, ffn2_w_down, ln3_g, ln3_b):
    bsz, seq, d = x.shape
    assert ln1_g.shape[0] == DEPTH and hg_lb_logits.shape == (DEPTH + 1, MIX_W)
    n = bsz * seq
    tm = _pick(n, 512)
    tf = _pick(ffn1_w_gate.shape[2], 512)
    tt = _pick(seq, 256)
    tq = _pick(seq, 256)
    bf = lambda a: a[0].astype(BF16)
    row = lambda a: a[0].reshape(1, -1)

    w1 = (bf(ffn1_w_gate), bf(ffn1_w_up), bf(ffn1_w_down), row(ln1_g), row(ln1_b))
    w2 = (bf(ffn2_w_gate), bf(ffn2_w_up), bf(ffn2_w_down), row(ln3_g), row(ln3_b))
    w_in_b = bf(w_in)
    bg = row(b_gate)
    gain = row(hg_norm_g)

    hp = jnp.concatenate([jnp.zeros((PAD, d), x.dtype), meta.astype(x.dtype)], axis=0)
    hp = _ffn_ln(hp, *w1, tm=BLOCK, tf=tf)
    _, lf_p, v_p, _, _, sk_p, sv_p, _ = _proj(hp, w_in_b, bg, hg_lb_logits, tm=BLOCK, n_inert=PAD)
    zeros_p = jnp.zeros((1, BLOCK, MIX_W), BF16)
    _, st0 = _hgrn(zeros_p, lf_p[None], v_p[None], zeros_p, gain,
                   jnp.zeros((HEADS, HEAD_DIM, HEAD_DIM), F32), tt=BLOCK, chunk=64)
    kp = jnp.pad(sk_p, ((0, tq - BLOCK), (0, 0)))
    vp = jnp.pad(sv_p, ((0, tq - BLOCK), (0, 0)))

    h = _ffn_ln(x.reshape(n, d), *w1, tm=tm, tf=tf)
    q_hg, lf, v_hg, og, sq, sk, sv, gates = _proj(h, w_in_b, bg, hg_lb_logits, tm=tm, n_inert=0)
    seq3 = lambda a: a.reshape(bsz, seq, MIX_W)
    o_hg, _ = _hgrn(seq3(q_hg), seq3(lf), seq3(v_hg), seq3(og), gain, st0[0], tt=tt, chunk=64)
    o_sb = _stick_breaking(seq3(sq), seq3(sk), seq3(sv), kp, vp, tq=tq)
    h = _merge_ln(h, o_hg.reshape(n, MIX_W), o_sb.reshape(n, MIX_W), gates, bf(w_proj_hg), bf(w_proj_sb),
                  bf(w_out), row(ln2_g), row(ln2_b), tm=_pick(n, 256))
    h = _ffn_ln(h, *w2, tm=tm, tf=tf)
    return h.reshape(bsz, seq, d)
```

```python
import functools
import math

import jax
import jax.numpy as jnp
from jax import lax
from jax.experimental import pallas as pl
from jax.experimental.pallas import tpu as pltpu

N_META = 16
BLOCK = 128
PAD = BLOCK - N_META
HEADS = 8
HEAD_DIM = 128
MIX_W = HEADS * HEAD_DIM
SUB = 8
SB_TILE = 256
LN_EPS = 1e-5
RMS_EPS = 1e-6
DEPTH = 1
DN_ALPHA = (2.0 * DEPTH) ** 0.25
SB_DEAD_LOG = -104.0
MXU_COLS = 256
VMEM_LIMIT = 56 * 1024 * 1024

F32 = jnp.float32
BF16 = jnp.bfloat16


def _layer_norm(y, g, b):
    mu = jnp.mean(y, axis=-1, keepdims=True)
    yc = y - mu
    var = jnp.mean(yc * yc, axis=-1, keepdims=True)
    return yc * lax.rsqrt(var + LN_EPS) * g + b


def _dot(a, b):
    return jnp.dot(a, b, preferred_element_type=F32)


def _dot_nt(a, b):
    return lax.dot_general(a, b, (((1,), (1,)), ((), ())), preferred_element_type=F32)


def _dot_tn(a, b):
    return lax.dot_general(a, b, (((0,), (0,)), ((), ())), preferred_element_type=F32)


def _split_bf16(x):
    hi = x.astype(BF16)
    return hi, (x - hi.astype(F32)).astype(BF16)


def _ffn_ln_kernel(x_ref, wg_ref, wu_ref, wd_ref, g_ref, b_ref, o_ref, *rest):
    ob_ref = rest[0] if len(rest) == 3 else None
    xb_ref, acc_ref = rest[-2:]
    j = pl.program_id(1)

    @pl.when(j == 0)
    def _():
        xb_ref[...] = x_ref[...].astype(BF16)
        acc_ref[...] = jnp.zeros_like(acc_ref)

    xb = xb_ref[...]
    gate = _dot(xb, wg_ref[...])
    up = _dot(xb, wu_ref[...])
    act = (gate * jax.nn.sigmoid(gate) * up).astype(BF16)
    acc_ref[...] += _dot(act, wd_ref[...])

    @pl.when(j == pl.num_programs(1) - 1)
    def _():
        y = _layer_norm(DN_ALPHA * x_ref[...] + 0.5 * acc_ref[...], g_ref[...], b_ref[...])
        o_ref[...] = y
        if ob_ref is not None:
            ob_ref[...] = y.astype(BF16)


def _ffn_ln(x, wg, wu, wd, g, b, *, tm, tf, emit_bf16):
    n, d = x.shape
    dff = wg.shape[1]
    assert n % tm == 0 and dff % tf == 0
    rows = pl.BlockSpec((tm, d), lambda i, j: (i, 0))
    out_shape = [jax.ShapeDtypeStruct((n, d), F32)]
    if emit_bf16:
        out_shape.append(jax.ShapeDtypeStruct((n, d), BF16))
    return pl.pallas_call(
        _ffn_ln_kernel,
        grid=(n // tm, dff // tf),
        in_specs=[
            rows,
            pl.BlockSpec((d, tf), lambda i, j: (0, j)),
            pl.BlockSpec((d, tf), lambda i, j: (0, j)),
            pl.BlockSpec((tf, d), lambda i, j: (j, 0)),
            pl.BlockSpec((1, d), lambda i, j: (0, 0)),
            pl.BlockSpec((1, d), lambda i, j: (0, 0)),
        ],
        out_specs=[rows] * len(out_shape),
        out_shape=out_shape,
        scratch_shapes=[pltpu.VMEM((tm, d), BF16), pltpu.VMEM((tm, d), F32)],
        compiler_params=pltpu.CompilerParams(
            dimension_semantics=("parallel", "arbitrary"), vmem_limit_bytes=VMEM_LIMIT),
        name="ffn_ln",
    )(x, wg, wu, wd, g, b)


def _proj_kernel(x_ref, w_ref, *rest, act, n_inert, tm):
    o_ref = rest[-1]
    x = x_ref[...]
    for c in range(MIX_W // MXU_COLS):
        cols = slice(c * MXU_COLS, (c + 1) * MXU_COLS)
        acc = _dot(x, w_ref[:, cols])
        if act == "silu":
            y = acc * jax.nn.sigmoid(acc)
        elif act == "gate":
            y = jax.nn.sigmoid(acc + rest[0][:, cols])
        elif act == "log_forget":
            l0 = rest[0][0:1, cols]
            l1 = rest[0][1:2, cols]
            m = jnp.maximum(l0, l1)
            e0 = jnp.exp(l0 - m)
            e1 = jnp.exp(l1 - m)
            lb = e0 / (e0 + e1)
            y = jnp.log(lb + (1.0 - lb) * jax.nn.sigmoid(acc))
            if n_inert:
                row = pl.program_id(0) * tm + lax.broadcasted_iota(jnp.int32, y.shape, 0)
                y = jnp.where(row >= n_inert, y, 0.0)
        else:
            y = acc
        o_ref[:, cols] = y.astype(o_ref.dtype)


def _proj(xb, w_in, extra, *, act, col_block, n_blocks, out_dtype, tm, n_inert=0):
    n, d = xb.shape
    assert n % tm == 0
    in_specs = [pl.BlockSpec((tm, d), lambda i, j: (i, 0)),
                pl.BlockSpec((d, MIX_W), lambda i, j: (0, col_block(j)))]
    args = [xb, w_in]
    if act == "gate":
        in_specs.append(pl.BlockSpec((1, MIX_W), lambda i, j: (0, j)))
        args.append(extra)
    elif act == "log_forget":
        in_specs.append(pl.BlockSpec(extra.shape, lambda i, j: (0, 0)))
        args.append(extra)
    return pl.pallas_call(
        functools.partial(_proj_kernel, act=act, n_inert=n_inert, tm=tm),
        grid=(n // tm, n_blocks),
        in_specs=in_specs,
        out_specs=pl.BlockSpec((tm, MIX_W), lambda i, j: (i, j)),
        out_shape=jax.ShapeDtypeStruct((n, n_blocks * MIX_W), out_dtype),
        compiler_params=pltpu.CompilerParams(
            dimension_semantics=("parallel", "arbitrary"), vmem_limit_bytes=VMEM_LIMIT),
        name="proj_" + act,
    )(*args)


def _proj_all(xb, w_in, b_gate, lb_logits, *, tm, n_inert):
    d = xb.shape[1]
    silu = _proj(xb, w_in, None, act="silu", col_block=lambda j: 3 * j, n_blocks=2, out_dtype=BF16, tm=tm)
    log_f = _proj(xb, w_in, lb_logits, act="log_forget", col_block=lambda j: 1, n_blocks=1, out_dtype=F32,
                  tm=tm, n_inert=n_inert)
    plain = _proj(xb, w_in, None, act="none", col_block=lambda j: jnp.where(j > 0, j + 3, 2), n_blocks=4,
                  out_dtype=BF16, tm=tm)
    gates = _proj(xb, w_in, b_gate, act="gate", col_block=lambda j: j + 7, n_blocks=2 * d // MIX_W,
                  out_dtype=BF16, tm=tm)
    return silu, log_f, plain, gates


def _hgrn_levels(tt):
    return [m for m in (16, 32, 64, 128, 256, 512) if m <= tt]


def _hgrn_kernel(q_ref, lf_ref, v_ref, og_ref, gain_ref, st0_ref, tri_ref, lvl_ref, o_ref, stf_ref, st_ref,
                 fpad_ref, kpad_ref, vpad_ref, *, tt):
    t_id = pl.program_id(2)

    @pl.when(t_id == 0)
    def _():
        st_ref[...] = st0_ref[...]

    q = q_ref[...].astype(F32)
    g = lf_ref[...]
    vb = v_ref[...]
    v = vb.astype(F32)
    f = jnp.exp(g)
    kk = 1.0 - f

    g_hi, g_lo = _split_bf16(g)
    b = _dot(tri_ref[...], jnp.concatenate([g_hi, g_lo], axis=0))

    row = lax.broadcasted_iota(jnp.int32, (tt, HEAD_DIM), 0)
    zeros = jnp.zeros((SUB, HEAD_DIM), F32)
    fpad_ref[0:SUB, :] = zeros
    kpad_ref[0:SUB, :] = zeros
    vpad_ref[0:SUB, :] = zeros
    fpad_ref[SUB:, :] = jnp.where((row & (SUB - 1)) == 0, 0.0, f)
    kpad_ref[SUB:, :] = kk
    vpad_ref[SUB:, :] = v
    o = jnp.sum(q * kk, axis=-1, keepdims=True) * v
    u = q
    for d in range(1, SUB):
        u = u * fpad_ref[pl.ds(SUB - d + 1, tt), :]
        k_d = kpad_ref[pl.ds(SUB - d, tt), :]
        v_d = vpad_ref[pl.ds(SUB - d, tt), :]
        o = o + jnp.sum(u * k_d, axis=-1, keepdims=True) * v_d

    lvl = lvl_ref[...]
    scores = jnp.zeros((tt, tt), F32)
    for idx, m in enumerate(_hgrn_levels(tt)):
        mid = b.reshape(tt // m, m, HEAD_DIM)[:, m // 2 - 1:m // 2, :]
        b_mid = jnp.broadcast_to(mid, (tt // m, m, HEAD_DIM)).reshape(tt, HEAD_DIM)
        e = jnp.exp(-jnp.abs(b - b_mid))
        scores = jnp.where(lvl == idx, _dot_nt((q * e).astype(BF16), (kk * e).astype(BF16)), scores)
    st = st_ref[...]
    o = o + _dot(scores.astype(BF16), vb) + _dot_nt((q * jnp.exp(b)).astype(BF16), st.astype(BF16))

    b_end = b[tt - 1:tt, :]
    k_end = (kk * jnp.exp(b_end - b)).astype(BF16)
    st = st * jnp.exp(b_end) + _dot_tn(vb, k_end)
    st_ref[...] = st

    o = o * lax.rsqrt(jnp.mean(o * o, axis=-1, keepdims=True) + RMS_EPS)
    o_ref[...] = (o * gain_ref[...] * og_ref[...].astype(F32)).astype(BF16)

    @pl.when(t_id == pl.num_programs(2) - 1)
    def _():
        stf_ref[...] = st


def _hgrn(q_og, lf, plain, gain, st0, *, tt):
    bsz, length, _ = lf.shape
    levels = _hgrn_levels(tt)
    assert length % tt == 0 and levels[-1] == tt and SUB * 2 == levels[0]
    r = lax.broadcasted_iota(jnp.int32, (tt, 2 * tt), 0)
    c = lax.broadcasted_iota(jnp.int32, (tt, 2 * tt), 1) % tt
    tri = (c <= r).astype(BF16)
    r = lax.broadcasted_iota(jnp.int32, (tt, tt), 0)
    c = lax.broadcasted_iota(jnp.int32, (tt, tt), 1)
    lvl = jnp.full((tt, tt), -1, jnp.int32)
    for idx, m in reversed(list(enumerate(levels))):
        lvl = jnp.where(r // m == c // m, idx, lvl)
    lvl = jnp.where(jnp.logical_or(r // SUB == c // SUB, c > r), -1, lvl)
    tok = lambda off: pl.BlockSpec((None, tt, HEAD_DIM), lambda b, h, t: (b, t, h + off))
    const = lambda a: pl.BlockSpec(a.shape, lambda b, h, t: (0, 0))
    return pl.pallas_call(
        functools.partial(_hgrn_kernel, tt=tt),
        grid=(bsz, HEADS, length // tt),
        in_specs=[tok(0), tok(0), tok(0), tok(HEADS),
                  pl.BlockSpec((1, HEAD_DIM), lambda b, h, t: (0, h)),
                  pl.BlockSpec((None, HEAD_DIM, HEAD_DIM), lambda b, h, t: (h, 0, 0)),
                  const(tri), const(lvl)],
        out_specs=[tok(0), pl.BlockSpec((None, None, HEAD_DIM, HEAD_DIM), lambda b, h, t: (b, h, 0, 0))],
        out_shape=[jax.ShapeDtypeStruct((bsz, length, MIX_W), BF16),
                   jax.ShapeDtypeStruct((bsz, HEADS, HEAD_DIM, HEAD_DIM), F32)],
        scratch_shapes=[pltpu.VMEM((HEAD_DIM, HEAD_DIM), F32)] + [pltpu.VMEM((tt + SUB, HEAD_DIM), F32)] * 3,
        compiler_params=pltpu.CompilerParams(
            dimension_semantics=("parallel", "parallel", "arbitrary"), vmem_limit_bytes=VMEM_LIMIT),
        name="hgrn2",
    )(q_og, lf, plain, q_og, gain, st0, tri, lvl)


def _sb_kernel(q_ref, k_ref, v_ref, kp_ref, vp_ref, tri_ref, o_ref, acc_ref, rem_ref, *, ts, n_sub):
    qi = pl.program_id(2)
    scale = 1.0 / math.sqrt(HEAD_DIM)
    r_i = lax.broadcasted_iota(jnp.int32, (ts, ts), 0)
    c_i = lax.broadcasted_iota(jnp.int32, (ts, ts), 1)
    causal = c_i < r_i
    prefix_valid = jnp.logical_and(c_i >= PAD, c_i < BLOCK)

    def tile(j):
        start = pl.multiple_of(j * ts, ts)
        return k_ref[pl.ds(start, ts), :], v_ref[pl.ds(start, ts), :]

    def log_weights(q, k_t, mask):
        z = _dot_nt(q, k_t) * scale
        log_beta = jnp.minimum(z, 0.0) - jnp.log(1.0 + jnp.exp(-jnp.abs(z)))
        log_rest = log_beta - z
        if mask is not None:
            log_rest = jnp.where(mask, log_rest, 0.0)
        hi, lo = _split_bf16(log_rest)
        later = _dot(jnp.concatenate([hi, lo], axis=1), tri_ref[...])
        return log_beta + later, jnp.sum(log_rest, axis=-1, keepdims=True)

    def weights(log_w, mask):
        w = jnp.exp(log_w)
        if mask is not None:
            w = jnp.where(mask, w, 0.0)
        return w.astype(BF16)

    def first_visits(sub0_has_prev):
        for s in range(n_sub):
            q = q_ref[s * ts:(s + 1) * ts, :]
            g = qi * n_sub + s
            k_t, v_t = tile(g)
            log_w, rem = log_weights(q, k_t, causal)
            acc = _dot(weights(log_w, causal), v_t)
            if s > 0 or sub0_has_prev:
                k_t, v_t = tile(g - 1)
                log_w, total = log_weights(q, k_t, None)
                acc = acc + _dot(weights(log_w + rem, None), v_t)
                rem = rem + total
            acc_ref[s] = acc
            rem_ref[s] = rem

    @pl.when(qi > 0)
    def _():
        first_visits(True)

    @pl.when(qi == 0)
    def _():
        first_visits(False)

    for s in range(n_sub):
        q = q_ref[s * ts:(s + 1) * ts, :]

        def visit(k_t, v_t, mask):
            log_w, total = log_weights(q, k_t, mask)
            rem = rem_ref[s]
            acc_ref[s] += _dot(weights(log_w + rem, mask), v_t)
            rem = rem + total
            rem_ref[s] = rem
            return jnp.max(rem)

        def cond(state):
            j, live = state
            return jnp.logical_and(j >= 0, live > SB_DEAD_LOG)

        def body(state):
            j, _ = state
            return j - 1, visit(*tile(j), None)

        _, live = lax.while_loop(cond, body, (qi * n_sub + s - 2, jnp.max(rem_ref[s])))

        @pl.when(live > SB_DEAD_LOG)
        def _():
            visit(kp_ref[...], vp_ref[...], prefix_valid)

        o_ref[s * ts:(s + 1) * ts, :] = acc_ref[s].astype(BF16)


def _stick_breaking(plain, kp, vp, *, n_sub):
    bsz, length, _ = plain.shape
    ts = SB_TILE
    tq = ts * n_sub
    assert length % tq == 0
    r = lax.broadcasted_iota(jnp.int32, (2 * ts, ts), 0) % ts
    c = lax.broadcasted_iota(jnp.int32, (2 * ts, ts), 1)
    tri = (r > c).astype(BF16)
    seq = lambda off: pl.BlockSpec((None, length, HEAD_DIM), lambda b, h, i: (b, 0, h + off))
    pre = pl.BlockSpec((ts, HEAD_DIM), lambda b, h, i: (0, h))
    return pl.pallas_call(
        functools.partial(_sb_kernel, ts=ts, n_sub=n_sub),
        grid=(bsz, HEADS, length // tq),
        in_specs=[pl.BlockSpec((None, tq, HEAD_DIM), lambda b, h, i: (b, i, h + HEADS)),
                  seq(2 * HEADS), seq(3 * HEADS), pre, pre,
                  pl.BlockSpec((2 * ts, ts), lambda b, h, i: (0, 0))],
        out_specs=pl.BlockSpec((None, tq, HEAD_DIM), lambda b, h, i: (b, i, h)),
        out_shape=jax.ShapeDtypeStruct((bsz, length, MIX_W), BF16),
        scratch_shapes=[pltpu.VMEM((n_sub, ts, HEAD_DIM), F32), pltpu.VMEM((n_sub, ts, 1), F32)],
        compiler_params=pltpu.CompilerParams(
            dimension_semantics=("parallel", "parallel", "arbitrary"), vmem_limit_bytes=VMEM_LIMIT),
        name="stick_breaking",
    )(plain, plain, plain, kp, vp, tri)


def _merge_ln_kernel(h_ref, ohg_ref, osb_ref, ghg_ref, gsb_ref, php_ref, psb_ref, wo_ref, g_ref, b_ref, o_ref):
    y = (ghg_ref[...].astype(F32) * _dot(ohg_ref[...], php_ref[...])
         + gsb_ref[...].astype(F32) * _dot(osb_ref[...], psb_ref[...]))
    mix = _dot(y.astype(BF16), wo_ref[...])
    o_ref[...] = _layer_norm(DN_ALPHA * h_ref[...] + mix, g_ref[...], b_ref[...])


def _merge_ln(h, o_hg, o_sb, gates, p_hg, p_sb, w_out, g, b, *, tm):
    n, d = h.shape
    assert n % tm == 0
    row = lambda w: pl.BlockSpec((tm, w), lambda i: (i, 0))
    full = lambda a: pl.BlockSpec(a.shape, lambda i: (0, 0), pipeline_mode=pl.Buffered(1))
    return pl.pallas_call(
        _merge_ln_kernel,
        grid=(n // tm,),
        in_specs=[row(d), row(MIX_W), row(MIX_W),
                  pl.BlockSpec((tm, d), lambda i: (i, 0)), pl.BlockSpec((tm, d), lambda i: (i, 1)),
                  full(p_hg), full(p_sb), full(w_out), full(g), full(b)],
        out_specs=row(d),
        out_shape=jax.ShapeDtypeStruct((n, d), F32),
        compiler_params=pltpu.CompilerParams(
            dimension_semantics=("parallel",), vmem_limit_bytes=VMEM_LIMIT),
        name="merge_ln",
    )(h, o_hg, o_sb, gates, gates, p_hg, p_sb, w_out, g, b)


def _pick(n, pref):
    t = min(n, pref)
    while n % t:
        t //= 2
    return t


def kernel(x, meta, ln1_g, ln1_b, ffn1_w_gate, ffn1_w_up, ffn1_w_down, w_in, b_gate, hg_lb_logits, hg_norm_g, w_proj_hg, w_proj_sb, w_out, ln2_g, ln2_b, ffn2_w_gate, ffn2_w_up, ffn2_w_down, ln3_g, ln3_b):
    bsz, seq, d = x.shape
    assert ln1_g.shape[0] == DEPTH and hg_lb_logits.shape == (DEPTH + 1, MIX_W) and (2 * d) % MIX_W == 0
    n = bsz * seq
    tm = _pick(n, 512)
    tf = _pick(ffn1_w_gate.shape[2], 512)
    bf = lambda a: a[0].astype(BF16)
    row = lambda a: a[0].reshape(1, -1)

    w1 = (bf(ffn1_w_gate), bf(ffn1_w_up), bf(ffn1_w_down), row(ln1_g), row(ln1_b))
    w2 = (bf(ffn2_w_gate), bf(ffn2_w_up), bf(ffn2_w_down), row(ln3_g), row(ln3_b))
    w_in_b = bf(w_in)
    bg = row(b_gate)
    gain = row(hg_norm_g)

    hp = jnp.concatenate([jnp.zeros((PAD, d), x.dtype), meta.astype(x.dtype)], axis=0)
    _, hpb = _ffn_ln(hp, *w1, tm=BLOCK, tf=tf, emit_bf16=True)
    _, lf_p, plain_p, _ = _proj_all(hpb, w_in_b, bg, hg_lb_logits, tm=BLOCK, n_inert=PAD)
    _, st0 = _hgrn(jnp.zeros((1, BLOCK, 2 * MIX_W), BF16), lf_p[None], plain_p[None], gain,
                   jnp.zeros((HEADS, HEAD_DIM, HEAD_DIM), F32), tt=BLOCK)
    pad_rows = ((0, SB_TILE - BLOCK), (0, 0))
    kp = jnp.pad(plain_p[:, 2 * MIX_W:3 * MIX_W], pad_rows)
    vp = jnp.pad(plain_p[:, 3 * MIX_W:], pad_rows)

    h, hb = _ffn_ln(x.reshape(n, d), *w1, tm=tm, tf=tf, emit_bf16=True)
    q_og, lf, plain, gates = _proj_all(hb, w_in_b, bg, hg_lb_logits, tm=tm, n_inert=0)
    seq3 = lambda a: a.reshape(bsz, seq, a.shape[-1])
    o_hg, _ = _hgrn(seq3(q_og), seq3(lf), seq3(plain), gain, st0[0], tt=_pick(seq, 256))
    o_sb = _stick_breaking(seq3(plain), kp, vp, n_sub=2 if seq % (2 * SB_TILE) == 0 else 1)
    h = _merge_ln(h, o_hg.reshape(n, MIX_W), o_sb.reshape(n, MIX_W), gates, bf(w_proj_hg), bf(w_proj_sb),
                  bf(w_out), row(ln2_g), row(ln2_b), tm=_pick(n, 512))
    (h,) = _ffn_ln(h, *w2, tm=tm, tf=tf, emit_bf16=False)
    return h.reshape(bsz, seq, d)
```

```python
import functools
import math

import jax
import jax.numpy as jnp
from jax import lax
from jax.experimental import pallas as pl
from jax.experimental.pallas import tpu as pltpu

N_META = 16
BLOCK = 128
PAD = BLOCK - N_META
HEADS = 8
HEAD_DIM = 128
MIX_W = HEADS * HEAD_DIM
SUB = 8
HG_HEADS_PER_STEP = 2
SB_TILE = 256
LN_EPS = 1e-5
RMS_EPS = 1e-6
DEPTH = 1
DN_ALPHA = (2.0 * DEPTH) ** 0.25
SB_DEAD_LOG = -104.0
MXU_COLS = 256
VMEM_LIMIT = 60 * 1024 * 1024

F32 = jnp.float32
BF16 = jnp.bfloat16


def _layer_norm(y, g, b):
    mu = jnp.mean(y, axis=-1, keepdims=True)
    yc = y - mu
    var = jnp.mean(yc * yc, axis=-1, keepdims=True)
    return yc * lax.rsqrt(var + LN_EPS) * g + b


def _dot(a, b):
    return jnp.dot(a, b, preferred_element_type=F32)


def _dot_nt(a, b):
    return lax.dot_general(a, b, (((1,), (1,)), ((), ())), preferred_element_type=F32)


def _dot_tn(a, b):
    return lax.dot_general(a, b, (((0,), (0,)), ((), ())), preferred_element_type=F32)


def _split_bf16(x):
    hi = x.astype(BF16)
    return hi, (x - hi.astype(F32)).astype(BF16)


def _ffn_ln_kernel(x_ref, wg_ref, wu_ref, wd_ref, g_ref, b_ref, o_ref, *rest, emit_bf16):
    (xb_ref,) = rest
    ob_ref = xb_ref if emit_bf16 else None
    j = pl.program_id(1)

    @pl.when(j == 0)
    def _():
        x = x_ref[...]
        xb_ref[...] = x.astype(BF16)
        o_ref[...] = (DN_ALPHA / 0.5) * x

    xb = xb_ref[...]
    gate = _dot(xb, wg_ref[...])
    up = _dot(xb, wu_ref[...])
    act = (gate * jax.nn.sigmoid(gate) * up).astype(BF16)
    o_ref[...] += _dot(act, wd_ref[...])

    @pl.when(j == pl.num_programs(1) - 1)
    def _():
        y = _layer_norm(0.5 * o_ref[...], g_ref[...], b_ref[...])
        o_ref[...] = y
        if ob_ref is not None:
            ob_ref[...] = y.astype(BF16)


def _ffn_ln(x, wg, wu, wd, g, b, *, tm, tf, emit_bf16):
    n, d = x.shape
    dff = wg.shape[1]
    assert n % tm == 0 and dff % tf == 0
    rows = pl.BlockSpec((tm, d), lambda i, j: (i, 0))
    out_shape = [jax.ShapeDtypeStruct((n, d), F32)]
    if emit_bf16:
        out_shape.append(jax.ShapeDtypeStruct((n, d), BF16))
    return pl.pallas_call(
        functools.partial(_ffn_ln_kernel, emit_bf16=emit_bf16),
        grid=(n // tm, dff // tf),
        in_specs=[
            pl.BlockSpec((tm, d), lambda i, j: (i, 0), pipeline_mode=pl.Buffered(1)),
            pl.BlockSpec((d, tf), lambda i, j: (0, j)),
            pl.BlockSpec((d, tf), lambda i, j: (0, j)),
            pl.BlockSpec((tf, d), lambda i, j: (j, 0)),
            pl.BlockSpec((1, d), lambda i, j: (0, 0)),
            pl.BlockSpec((1, d), lambda i, j: (0, 0)),
        ],
        out_specs=[rows] * len(out_shape),
        out_shape=out_shape,
        scratch_shapes=[] if emit_bf16 else [pltpu.VMEM((tm, d), BF16)],
        compiler_params=pltpu.CompilerParams(
            dimension_semantics=("parallel", "arbitrary"), vmem_limit_bytes=VMEM_LIMIT),
        name="ffn_ln",
    )(x, wg, wu, wd, g, b)


def _proj_kernel(x_ref, w_ref, *rest, act, n_inert, tm):
    o_ref = rest[-1]
    x = x_ref[...]
    for c in range(MIX_W // MXU_COLS):
        cols = slice(c * MXU_COLS, (c + 1) * MXU_COLS)
        acc = _dot(x, w_ref[:, cols])
        if act == "silu":
            y = acc * jax.nn.sigmoid(acc)
        elif act == "gate":
            y = jax.nn.sigmoid(acc + rest[0][:, cols])
        elif act == "log_forget":
            l0 = rest[0][0:1, cols]
            l1 = rest[0][1:2, cols]
            m = jnp.maximum(l0, l1)
            e0 = jnp.exp(l0 - m)
            e1 = jnp.exp(l1 - m)
            lb = e0 / (e0 + e1)
            y = jnp.log(lb + (1.0 - lb) * jax.nn.sigmoid(acc))
            if n_inert:
                row = pl.program_id(0) * tm + lax.broadcasted_iota(jnp.int32, y.shape, 0)
                y = jnp.where(row >= n_inert, y, 0.0)
        else:
            y = acc
        o_ref[:, cols] = y.astype(o_ref.dtype)


def _proj(xb, w_in, extra, *, act, col_block, n_blocks, out_dtype, tm, n_inert=0):
    n, d = xb.shape
    assert n % tm == 0
    in_specs = [pl.BlockSpec((tm, d), lambda i, j: (i, 0)),
                pl.BlockSpec((d, MIX_W), lambda i, j: (0, col_block(j)))]
    args = [xb, w_in]
    if act == "gate":
        in_specs.append(pl.BlockSpec((1, MIX_W), lambda i, j: (0, j)))
        args.append(extra)
    elif act == "log_forget":
        in_specs.append(pl.BlockSpec(extra.shape, lambda i, j: (0, 0)))
        args.append(extra)
    return pl.pallas_call(
        functools.partial(_proj_kernel, act=act, n_inert=n_inert, tm=tm),
        grid=(n // tm, n_blocks),
        in_specs=in_specs,
        out_specs=pl.BlockSpec((tm, MIX_W), lambda i, j: (i, j)),
        out_shape=jax.ShapeDtypeStruct((n, n_blocks * MIX_W), out_dtype),
        compiler_params=pltpu.CompilerParams(
            dimension_semantics=("parallel", "arbitrary"), vmem_limit_bytes=VMEM_LIMIT),
        name="proj_" + act,
    )(*args)


def _proj_all(xb, w_in, b_gate, lb_logits, *, tm, n_inert):
    d = xb.shape[1]
    silu = _proj(xb, w_in, None, act="silu", col_block=lambda j: 3 * j, n_blocks=2, out_dtype=BF16, tm=tm)
    log_f = _proj(xb, w_in, lb_logits, act="log_forget", col_block=lambda j: 1, n_blocks=1, out_dtype=F32,
                  tm=tm, n_inert=n_inert)
    plain = _proj(xb, w_in, None, act="none", col_block=lambda j: jnp.where(j > 0, j + 3, 2), n_blocks=4,
                  out_dtype=BF16, tm=tm)
    gates = _proj(xb, w_in, b_gate, act="gate", col_block=lambda j: j + 7, n_blocks=2 * d // MIX_W,
                  out_dtype=BF16, tm=tm)
    return silu, log_f, plain, gates


def _hgrn_levels(tt):
    return [m for m in (16, 32, 64, 128, 256, 512) if m <= tt]


def _hgrn_kernel(q_ref, lf_ref, v_ref, og_ref, gain_ref, st0_ref, tri_ref, lvl_ref, o_ref, stf_ref, st_ref,
                 fpad_ref, kpad_ref, vpad_ref, *, tt, hpb):
    t_id = pl.program_id(2)

    @pl.when(t_id == 0)
    def _():
        st_ref[...] = st0_ref[...]

    row = lax.broadcasted_iota(jnp.int32, (tt, HEAD_DIM), 0)
    sub_start = (row & (SUB - 1)) == 0
    zeros = jnp.zeros((SUB, HEAD_DIM), F32)
    lvl = lvl_ref[...]

    for hd in range(hpb):
        lanes = slice(hd * HEAD_DIM, (hd + 1) * HEAD_DIM)
        q = q_ref[:, lanes].astype(F32)
        g = lf_ref[:, lanes]
        vb = v_ref[:, lanes]
        v = vb.astype(F32)
        f = jnp.exp(g)
        kk = 1.0 - f

        g_hi, g_lo = _split_bf16(g)
        b = _dot(tri_ref[...], jnp.concatenate([g_hi, g_lo], axis=0))

        fpad_ref[hd, 0:SUB, :] = zeros
        kpad_ref[hd, 0:SUB, :] = zeros
        vpad_ref[hd, 0:SUB, :] = zeros
        fpad_ref[hd, SUB:, :] = jnp.where(sub_start, 0.0, f)
        kpad_ref[hd, SUB:, :] = kk
        vpad_ref[hd, SUB:, :] = v
        o = jnp.sum(q * kk, axis=-1, keepdims=True) * v
        u = q
        for d in range(1, SUB):
            u = u * fpad_ref[hd, pl.ds(SUB - d + 1, tt), :]
            k_d = kpad_ref[hd, pl.ds(SUB - d, tt), :]
            v_d = vpad_ref[hd, pl.ds(SUB - d, tt), :]
            o = o + jnp.sum(u * k_d, axis=-1, keepdims=True) * v_d

        scores = jnp.zeros((tt, tt), F32)
        for idx, m in enumerate(_hgrn_levels(tt)):
            mid = b.reshape(tt // m, m, HEAD_DIM)[:, m // 2 - 1:m // 2, :]
            b_mid = jnp.broadcast_to(mid, (tt // m, m, HEAD_DIM)).reshape(tt, HEAD_DIM)
            e = jnp.exp(-jnp.abs(b - b_mid))
            scores = jnp.where(lvl == idx, _dot_nt((q * e).astype(BF16), (kk * e).astype(BF16)), scores)
        st = st_ref[hd]
        o = o + _dot(scores.astype(BF16), vb) + _dot_nt((q * jnp.exp(b)).astype(BF16), st.astype(BF16))

        b_end = b[tt - 1:tt, :]
        k_end = (kk * jnp.exp(b_end - b)).astype(BF16)
        st_ref[hd] = st * jnp.exp(b_end) + _dot_tn(vb, k_end)

        o = o * lax.rsqrt(jnp.mean(o * o, axis=-1, keepdims=True) + RMS_EPS)
        o_ref[:, lanes] = (o * gain_ref[:, lanes] * og_ref[:, lanes].astype(F32)).astype(BF16)

    @pl.when(t_id == pl.num_programs(2) - 1)
    def _():
        stf_ref[...] = st_ref[...]


def _hgrn(q_og, lf, plain, gain, st0, *, tt, hpb):
    bsz, length, _ = lf.shape
    levels = _hgrn_levels(tt)
    assert length % tt == 0 and levels[-1] == tt and SUB * 2 == levels[0] and HEADS % hpb == 0
    r = lax.broadcasted_iota(jnp.int32, (tt, 2 * tt), 0)
    c = lax.broadcasted_iota(jnp.int32, (tt, 2 * tt), 1) % tt
    tri = (c <= r).astype(BF16)
    r = lax.broadcasted_iota(jnp.int32, (tt, tt), 0)
    c = lax.broadcasted_iota(jnp.int32, (tt, tt), 1)
    lvl = jnp.full((tt, tt), -1, jnp.int32)
    for idx, m in reversed(list(enumerate(levels))):
        lvl = jnp.where(r // m == c // m, idx, lvl)
    lvl = jnp.where(jnp.logical_or(r // SUB == c // SUB, c > r), -1, lvl)
    groups = HEADS // hpb
    width = hpb * HEAD_DIM
    tok = lambda off: pl.BlockSpec((None, tt, width), lambda b, h, t: (b, t, h + off))
    const = lambda a: pl.BlockSpec(a.shape, lambda b, h, t: (0, 0))
    return pl.pallas_call(
        functools.partial(_hgrn_kernel, tt=tt, hpb=hpb),
        grid=(bsz, groups, length // tt),
        in_specs=[tok(0), tok(0), tok(0), tok(groups),
                  pl.BlockSpec((1, width), lambda b, h, t: (0, h)),
                  pl.BlockSpec((hpb, HEAD_DIM, HEAD_DIM), lambda b, h, t: (h, 0, 0)),
                  const(tri), const(lvl)],
        out_specs=[tok(0), pl.BlockSpec((None, hpb, HEAD_DIM, HEAD_DIM), lambda b, h, t: (b, h, 0, 0))],
        out_shape=[jax.ShapeDtypeStruct((bsz, length, MIX_W), BF16),
                   jax.ShapeDtypeStruct((bsz, HEADS, HEAD_DIM, HEAD_DIM), F32)],
        scratch_shapes=[pltpu.VMEM((hpb, HEAD_DIM, HEAD_DIM), F32)]
        + [pltpu.VMEM((hpb, tt + SUB, HEAD_DIM), F32)] * 3,
        compiler_params=pltpu.CompilerParams(
            dimension_semantics=("parallel", "parallel", "arbitrary"), vmem_limit_bytes=VMEM_LIMIT),
        name="hgrn2",
    )(q_og, lf, plain, q_og, gain, st0, tri, lvl)


def _sb_kernel(q_ref, k_ref, v_ref, kp_ref, vp_ref, tri_ref, o_ref, acc_ref, rem_ref, *, ts, n_sub):
    qi = pl.program_id(2)
    scale = 1.0 / math.sqrt(HEAD_DIM)
    r_i = lax.broadcasted_iota(jnp.int32, (ts, ts), 0)
    c_i = lax.broadcasted_iota(jnp.int32, (ts, ts), 1)
    causal = c_i < r_i
    prefix_valid = jnp.logical_and(c_i >= PAD, c_i < BLOCK)

    def tile(j):
        start = pl.multiple_of(j * ts, ts)
        return k_ref[pl.ds(start, ts), :], v_ref[pl.ds(start, ts), :]

    def log_weights(q, k_t, mask):
        z = _dot_nt(q, k_t) * scale
        log_beta = jnp.minimum(z, 0.0) - jnp.log(1.0 + jnp.exp(-jnp.abs(z)))
        log_rest = log_beta - z
        if mask is not None:
            log_rest = jnp.where(mask, log_rest, 0.0)
        hi, lo = _split_bf16(log_rest)
        later = _dot(jnp.concatenate([hi, lo], axis=1), tri_ref[...])
        return log_beta + later, jnp.sum(log_rest, axis=-1, keepdims=True)

    def weights(log_w, mask):
        w = jnp.exp(log_w)
        if mask is not None:
            w = jnp.where(mask, w, 0.0)
        return w.astype(BF16)

    def first_visits(sub0_has_prev):
        for s in range(n_sub):
            q = q_ref[s * ts:(s + 1) * ts, :]
            g = qi * n_sub + s
            k_t, v_t = tile(g)
            log_w, rem = log_weights(q, k_t, causal)
            acc = _dot(weights(log_w, causal), v_t)
            if s > 0 or sub0_has_prev:
                k_t, v_t = tile(g - 1)
                log_w, total = log_weights(q, k_t, None)
                acc = acc + _dot(weights(log_w + rem, None), v_t)
                rem = rem + total
            acc_ref[s] = acc
            rem_ref[s] = rem

    @pl.when(qi > 0)
    def _():
        first_visits(True)

    @pl.when(qi == 0)
    def _():
        first_visits(False)

    for s in range(n_sub):
        q = q_ref[s * ts:(s + 1) * ts, :]

        def visit(k_t, v_t, mask):
            log_w, total = log_weights(q, k_t, mask)
            rem = rem_ref[s]
            acc_ref[s] += _dot(weights(log_w + rem, mask), v_t)
            rem = rem + total
            rem_ref[s] = rem
            return jnp.max(rem)

        def cond(state):
            j, live = state
            return jnp.logical_and(j >= 0, live > SB_DEAD_LOG)

        def body(state):
            j, _ = state
            return j - 1, visit(*tile(j), None)

        _, live = lax.while_loop(cond, body, (qi * n_sub + s - 2, jnp.max(rem_ref[s])))

        @pl.when(live > SB_DEAD_LOG)
        def _():
            visit(kp_ref[...], vp_ref[...], prefix_valid)

        o_ref[s * ts:(s + 1) * ts, :] = acc_ref[s].astype(BF16)


def _stick_breaking(plain, kp, vp, *, n_sub):
    bsz, length, _ = plain.shape
    ts = SB_TILE
    tq = ts * n_sub
    assert length % tq == 0
    r = lax.broadcasted_iota(jnp.int32, (2 * ts, ts), 0) % ts
    c = lax.broadcasted_iota(jnp.int32, (2 * ts, ts), 1)
    tri = (r > c).astype(BF16)
    seq = lambda off: pl.BlockSpec((None, length, HEAD_DIM), lambda b, h, i: (b, 0, h + off))
    pre = pl.BlockSpec((ts, HEAD_DIM), lambda b, h, i: (0, h))
    return pl.pallas_call(
        functools.partial(_sb_kernel, ts=ts, n_sub=n_sub),
        grid=(bsz, HEADS, length // tq),
        in_specs=[pl.BlockSpec((None, tq, HEAD_DIM), lambda b, h, i: (b, i, h + HEADS)),
                  seq(2 * HEADS), seq(3 * HEADS), pre, pre,
                  pl.BlockSpec((2 * ts, ts), lambda b, h, i: (0, 0))],
        out_specs=pl.BlockSpec((None, tq, HEAD_DIM), lambda b, h, i: (b, i, h)),
        out_shape=jax.ShapeDtypeStruct((bsz, length, MIX_W), BF16),
        scratch_shapes=[pltpu.VMEM((n_sub, ts, HEAD_DIM), F32), pltpu.VMEM((n_sub, ts, 1), F32)],
        compiler_params=pltpu.CompilerParams(
            dimension_semantics=("parallel", "parallel", "arbitrary"), vmem_limit_bytes=VMEM_LIMIT),
        name="stick_breaking",
    )(plain, plain, plain, kp, vp, tri)


def _merge_ln_kernel(h_ref, ohg_ref, osb_ref, ghg_ref, gsb_ref, php_ref, psb_ref, wo_ref, g_ref, b_ref, o_ref):
    y = (ghg_ref[...].astype(F32) * _dot(ohg_ref[...], php_ref[...])
         + gsb_ref[...].astype(F32) * _dot(osb_ref[...], psb_ref[...]))
    mix = _dot(y.astype(BF16), wo_ref[...])
    o_ref[...] = _layer_norm(DN_ALPHA * h_ref[...] + mix, g_ref[...], b_ref[...])


def _merge_ln(h, o_hg, o_sb, gates, p_hg, p_sb, w_out, g, b, *, tm):
    n, d = h.shape
    assert n % tm == 0
    row = lambda w: pl.BlockSpec((tm, w), lambda i: (i, 0))
    full = lambda a: pl.BlockSpec(a.shape, lambda i: (0, 0), pipeline_mode=pl.Buffered(1))
    return pl.pallas_call(
        _merge_ln_kernel,
        grid=(n // tm,),
        in_specs=[row(d), row(MIX_W), row(MIX_W),
                  pl.BlockSpec((tm, d), lambda i: (i, 0)), pl.BlockSpec((tm, d), lambda i: (i, 1)),
                  full(p_hg), full(p_sb), full(w_out), full(g), full(b)],
        out_specs=row(d),
        out_shape=jax.ShapeDtypeStruct((n, d), F32),
        compiler_params=pltpu.CompilerParams(
            dimension_semantics=("parallel",), vmem_limit_bytes=VMEM_LIMIT),
        name="merge_ln",
    )(h, o_hg, o_sb, gates, gates, p_hg, p_sb, w_out, g, b)


def _pick(n, pref):
    t = min(n, pref)
    while n % t:
        t //= 2
    return t


def kernel(x, meta, ln1_g, ln1_b, ffn1_w_gate, ffn1_w_up, ffn1_w_down, w_in, b_gate, hg_lb_logits, hg_norm_g, w_proj_hg, w_proj_sb, w_out, ln2_g, ln2_b, ffn2_w_gate, ffn2_w_up, ffn2_w_down, ln3_g, ln3_b):
    bsz, seq, d = x.shape
    assert ln1_g.shape[0] == DEPTH and hg_lb_logits.shape == (DEPTH + 1, MIX_W) and (2 * d) % MIX_W == 0
    n = bsz * seq
    tm = _pick(n, 1024)
    tf = _pick(ffn1_w_gate.shape[2], 512)
    bf = lambda a: a[0].astype(BF16)
    row = lambda a: a[0].reshape(1, -1)

    w1 = (bf(ffn1_w_gate), bf(ffn1_w_up), bf(ffn1_w_down), row(ln1_g), row(ln1_b))
    w2 = (bf(ffn2_w_gate), bf(ffn2_w_up), bf(ffn2_w_down), row(ln3_g), row(ln3_b))
    w_in_b = bf(w_in)
    bg = row(b_gate)
    gain = row(hg_norm_g)

    hp = jnp.concatenate([jnp.zeros((PAD, d), x.dtype), meta.astype(x.dtype)], axis=0)
    _, hpb = _ffn_ln(hp, *w1, tm=BLOCK, tf=tf, emit_bf16=True)
    _, lf_p, plain_p, _ = _proj_all(hpb, w_in_b, bg, hg_lb_logits, tm=BLOCK, n_inert=PAD)
    _, st0 = _hgrn(jnp.zeros((1, BLOCK, 2 * MIX_W), BF16), lf_p[None], plain_p[None], gain,
                   jnp.zeros((HEADS, HEAD_DIM, HEAD_DIM), F32), tt=BLOCK, hpb=HG_HEADS_PER_STEP)
    pad_rows = ((0, SB_TILE - BLOCK), (0, 0))
    kp = jnp.pad(plain_p[:, 2 * MIX_W:3 * MIX_W], pad_rows)
    vp = jnp.pad(plain_p[:, 3 * MIX_W:], pad_rows)

    h, hb = _ffn_ln(x.reshape(n, d), *w1, tm=tm, tf=tf, emit_bf16=True)
    q_og, lf, plain, gates = _proj_all(hb, w_in_b, bg, hg_lb_logits, tm=tm, n_inert=0)
    seq3 = lambda a: a.reshape(bsz, seq, a.shape[-1])
    o_hg, _ = _hgrn(seq3(q_og), seq3(lf), seq3(plain), gain, st0[0], tt=_pick(seq, 256),
                    hpb=HG_HEADS_PER_STEP)
    o_sb = _stick_breaking(seq3(plain), kp, vp, n_sub=_pick(seq // SB_TILE, 4))
    h = _merge_ln(h, o_hg.reshape(n, MIX_W), o_sb.reshape(n, MIX_W), gates, bf(w_proj_hg), bf(w_proj_sb),
                  bf(w_out), row(ln2_g), row(ln2_b), tm=_pick(n, 512))
    (h,) = _ffn_ln(h, *w2, tm=tm, tf=tf, emit_bf16=False)
    return h.reshape(bsz, seq, d)
```

```python
import functools
import math

import jax
import jax.numpy as jnp
from jax import lax
from jax.experimental import pallas as pl
from jax.experimental.pallas import tpu as pltpu

N_META = 16
BLOCK = 128
PAD = BLOCK - N_META
HEADS = 8
HEAD_DIM = 128
MIX_W = HEADS * HEAD_DIM
SUB = 8
HG_HEADS_PER_STEP = 2
SB_TILE = 256
LN_EPS = 1e-5
RMS_EPS = 1e-6
DEPTH = 1
DN_ALPHA = (2.0 * DEPTH) ** 0.25
SB_DEAD_LOG = -104.0
MXU_COLS = 256
VMEM_LIMIT = 60 * 1024 * 1024

F32 = jnp.float32
BF16 = jnp.bfloat16


def _layer_norm(y, g, b):
    mu = jnp.mean(y, axis=-1, keepdims=True)
    yc = y - mu
    var = jnp.mean(yc * yc, axis=-1, keepdims=True)
    return yc * lax.rsqrt(var + LN_EPS) * g + b


def _dot(a, b):
    return jnp.dot(a, b, preferred_element_type=F32)


def _dot_nt(a, b):
    return lax.dot_general(a, b, (((1,), (1,)), ((), ())), preferred_element_type=F32)


def _dot_tn(a, b):
    return lax.dot_general(a, b, (((0,), (0,)), ((), ())), preferred_element_type=F32)


def _split_bf16(x):
    hi = x.astype(BF16)
    return hi, (x - hi.astype(F32)).astype(BF16)


def _ffn_ln_kernel(x_hbm, wg_ref, wu_ref, wd_ref, g_ref, b_ref, o_ref, *rest, emit_bf16, tm):
    xb_ref, xbuf_ref, sem = rest
    ob_ref = xb_ref if emit_bf16 else None
    i = pl.program_id(0)
    j = pl.program_id(1)
    n_i = pl.num_programs(0)
    n_j = pl.num_programs(1)

    def x_copy(tile):
        return pltpu.make_async_copy(x_hbm.at[pl.ds(pl.multiple_of(tile * tm, tm), tm), :], xbuf_ref, sem)

    @pl.when(jnp.logical_and(i == 0, j == 0))
    def _():
        x_copy(0).start()

    @pl.when(j == 0)
    def _():
        x_copy(i).wait()
        x = xbuf_ref[...]
        xb_ref[...] = x.astype(BF16)
        o_ref[...] = (DN_ALPHA / 0.5) * x

    @pl.when(jnp.logical_and(j == jnp.minimum(1, n_j - 1), i + 1 < n_i))
    def _():
        x_copy(i + 1).start()

    xb = xb_ref[...]
    gate = _dot(xb, wg_ref[...])
    up = _dot(xb, wu_ref[...])
    act = (gate * jax.nn.sigmoid(gate) * up).astype(BF16)
    o_ref[...] += _dot(act, wd_ref[...])

    @pl.when(j == n_j - 1)
    def _():
        y = _layer_norm(0.5 * o_ref[...], g_ref[...], b_ref[...])
        o_ref[...] = y
        if ob_ref is not None:
            ob_ref[...] = y.astype(BF16)


def _ffn_ln(x, wg, wu, wd, g, b, *, tm, tf, emit_bf16):
    n, d = x.shape
    dff = wg.shape[1]
    assert n % tm == 0 and dff % tf == 0
    rows = pl.BlockSpec((tm, d), lambda i, j: (i, 0))
    out_shape = [jax.ShapeDtypeStruct((n, d), F32)]
    if emit_bf16:
        out_shape.append(jax.ShapeDtypeStruct((n, d), BF16))
    return pl.pallas_call(
        functools.partial(_ffn_ln_kernel, emit_bf16=emit_bf16, tm=tm),
        grid=(n // tm, dff // tf),
        in_specs=[
            pl.BlockSpec(memory_space=pl.ANY),
            pl.BlockSpec((d, tf), lambda i, j: (0, j)),
            pl.BlockSpec((d, tf), lambda i, j: (0, j)),
            pl.BlockSpec((tf, d), lambda i, j: (j, 0)),
            pl.BlockSpec((1, d), lambda i, j: (0, 0)),
            pl.BlockSpec((1, d), lambda i, j: (0, 0)),
        ],
        out_specs=[rows] * len(out_shape),
        out_shape=out_shape,
        scratch_shapes=([] if emit_bf16 else [pltpu.VMEM((tm, d), BF16)])
        + [pltpu.VMEM((tm, d), F32), pltpu.SemaphoreType.DMA(())],
        compiler_params=pltpu.CompilerParams(
            dimension_semantics=("arbitrary", "arbitrary"), vmem_limit_bytes=VMEM_LIMIT),
        name="ffn_ln",
    )(x, wg, wu, wd, g, b)


def _proj_kernel(x_ref, w_ref, *rest, act, n_inert, tm):
    o_ref = rest[-1]
    x = x_ref[...]
    for c in range(MIX_W // MXU_COLS):
        cols = slice(c * MXU_COLS, (c + 1) * MXU_COLS)
        acc = _dot(x, w_ref[:, cols])
        if act == "silu":
            y = acc * jax.nn.sigmoid(acc)
        elif act == "gate":
            y = jax.nn.sigmoid(acc + rest[0][:, cols])
        elif act == "log_forget":
            l0 = rest[0][0:1, cols]
            l1 = rest[0][1:2, cols]
            m = jnp.maximum(l0, l1)
            e0 = jnp.exp(l0 - m)
            e1 = jnp.exp(l1 - m)
            lb = e0 / (e0 + e1)
            y = jnp.log(lb + (1.0 - lb) * jax.nn.sigmoid(acc))
            if n_inert:
                row = pl.program_id(0) * tm + lax.broadcasted_iota(jnp.int32, y.shape, 0)
                y = jnp.where(row >= n_inert, y, 0.0)
        else:
            y = acc
        o_ref[:, cols] = y.astype(o_ref.dtype)


def _proj(xb, w_in, extra, *, act, col_block, n_blocks, out_dtype, tm, n_inert=0):
    n, d = xb.shape
    assert n % tm == 0
    in_specs = [pl.BlockSpec((tm, d), lambda i, j: (i, 0)),
                pl.BlockSpec((d, MIX_W), lambda i, j: (0, col_block(j)))]
    args = [xb, w_in]
    if act == "gate":
        in_specs.append(pl.BlockSpec((1, MIX_W), lambda i, j: (0, j)))
        args.append(extra)
    elif act == "log_forget":
        in_specs.append(pl.BlockSpec(extra.shape, lambda i, j: (0, 0)))
        args.append(extra)
    return pl.pallas_call(
        functools.partial(_proj_kernel, act=act, n_inert=n_inert, tm=tm),
        grid=(n // tm, n_blocks),
        in_specs=in_specs,
        out_specs=pl.BlockSpec((tm, MIX_W), lambda i, j: (i, j)),
        out_shape=jax.ShapeDtypeStruct((n, n_blocks * MIX_W), out_dtype),
        compiler_params=pltpu.CompilerParams(
            dimension_semantics=("parallel", "arbitrary"), vmem_limit_bytes=VMEM_LIMIT),
        name="proj_" + act,
    )(*args)


def _proj_all(xb, w_in, b_gate, lb_logits, *, tm, n_inert):
    d = xb.shape[1]
    silu = _proj(xb, w_in, None, act="silu", col_block=lambda j: 3 * j, n_blocks=2, out_dtype=BF16, tm=tm)
    log_f = _proj(xb, w_in, lb_logits, act="log_forget", col_block=lambda j: 1, n_blocks=1, out_dtype=F32,
                  tm=tm, n_inert=n_inert)
    plain = _proj(xb, w_in, None, act="none", col_block=lambda j: jnp.where(j > 0, j + 3, 2), n_blocks=4,
                  out_dtype=BF16, tm=tm)
    gates = _proj(xb, w_in, b_gate, act="gate", col_block=lambda j: j + 7, n_blocks=2 * d // MIX_W,
                  out_dtype=BF16, tm=tm)
    return silu, log_f, plain, gates


def _hgrn_levels(tt):
    return [m for m in (16, 32, 64, 128, 256, 512) if m <= tt]


def _hgrn_kernel(q_ref, lf_ref, v_ref, og_ref, gain_ref, st0_ref, tri_ref, lvl_ref, o_ref, stf_ref, st_ref,
                 fpad_ref, kpad_ref, vpad_ref, *, tt, hpb):
    t_id = pl.program_id(2)

    @pl.when(t_id == 0)
    def _():
        st_ref[...] = st0_ref[...]

    row = lax.broadcasted_iota(jnp.int32, (tt, HEAD_DIM), 0)
    sub_start = (row & (SUB - 1)) == 0
    zeros = jnp.zeros((SUB, HEAD_DIM), F32)
    lvl = lvl_ref[...]

    for hd in range(hpb):
        lanes = slice(hd * HEAD_DIM, (hd + 1) * HEAD_DIM)
        q = q_ref[:, lanes].astype(F32)
        g = lf_ref[:, lanes]
        vb = v_ref[:, lanes]
        v = vb.astype(F32)
        f = jnp.exp(g)
        kk = 1.0 - f

        g_hi, g_lo = _split_bf16(g)
        b = _dot(tri_ref[...], jnp.concatenate([g_hi, g_lo], axis=0))

        fpad_ref[hd, 0:SUB, :] = zeros
        kpad_ref[hd, 0:SUB, :] = zeros
        vpad_ref[hd, 0:SUB, :] = zeros
        fpad_ref[hd, SUB:, :] = jnp.where(sub_start, 0.0, f)
        kpad_ref[hd, SUB:, :] = kk
        vpad_ref[hd, SUB:, :] = v
        o = jnp.sum(q * kk, axis=-1, keepdims=True) * v
        u = q
        for d in range(1, SUB):
            u = u * fpad_ref[hd, pl.ds(SUB - d + 1, tt), :]
            k_d = kpad_ref[hd, pl.ds(SUB - d, tt), :]
            v_d = vpad_ref[hd, pl.ds(SUB - d, tt), :]
            o = o + jnp.sum(u * k_d, axis=-1, keepdims=True) * v_d

        scores = jnp.zeros((tt, tt), F32)
        for idx, m in enumerate(_hgrn_levels(tt)):
            mid = b.reshape(tt // m, m, HEAD_DIM)[:, m // 2 - 1:m // 2, :]
            b_mid = jnp.broadcast_to(mid, (tt // m, m, HEAD_DIM)).reshape(tt, HEAD_DIM)
            e = jnp.exp(-jnp.abs(b - b_mid))
            scores = jnp.where(lvl == idx, _dot_nt((q * e).astype(BF16), (kk * e).astype(BF16)), scores)
        st = st_ref[hd]
        o = o + _dot(scores.astype(BF16), vb) + _dot_nt((q * jnp.exp(b)).astype(BF16), st.astype(BF16))

        b_end = b[tt - 1:tt, :]
        k_end = (kk * jnp.exp(b_end - b)).astype(BF16)
        st_ref[hd] = st * jnp.exp(b_end) + _dot_tn(vb, k_end)

        o = o * lax.rsqrt(jnp.mean(o * o, axis=-1, keepdims=True) + RMS_EPS)
        o_ref[:, lanes] = (o * gain_ref[:, lanes] * og_ref[:, lanes].astype(F32)).astype(BF16)

    @pl.when(t_id == pl.num_programs(2) - 1)
    def _():
        stf_ref[...] = st_ref[...]


def _hgrn(q_og, lf, plain, gain, st0, *, tt, hpb):
    bsz, length, _ = lf.shape
    levels = _hgrn_levels(tt)
    assert length % tt == 0 and levels[-1] == tt and SUB * 2 == levels[0] and HEADS % hpb == 0
    r = lax.broadcasted_iota(jnp.int32, (tt, 2 * tt), 0)
    c = lax.broadcasted_iota(jnp.int32, (tt, 2 * tt), 1) % tt
    tri = (c <= r).astype(BF16)
    r = lax.broadcasted_iota(jnp.int32, (tt, tt), 0)
    c = lax.broadcasted_iota(jnp.int32, (tt, tt), 1)
    lvl = jnp.full((tt, tt), -1, jnp.int32)
    for idx, m in reversed(list(enumerate(levels))):
        lvl = jnp.where(r // m == c // m, idx, lvl)
    lvl = jnp.where(jnp.logical_or(r // SUB == c // SUB, c > r), -1, lvl)
    groups = HEADS // hpb
    width = hpb * HEAD_DIM
    tok = lambda off: pl.BlockSpec((None, tt, width), lambda b, h, t: (b, t, h + off))
    const = lambda a: pl.BlockSpec(a.shape, lambda b, h, t: (0, 0))
    return pl.pallas_call(
        functools.partial(_hgrn_kernel, tt=tt, hpb=hpb),
        grid=(bsz, groups, length // tt),
        in_specs=[tok(0), tok(0), tok(0), tok(groups),
                  pl.BlockSpec((1, width), lambda b, h, t: (0, h)),
                  pl.BlockSpec((hpb, HEAD_DIM, HEAD_DIM), lambda b, h, t: (h, 0, 0)),
                  const(tri), const(lvl)],
        out_specs=[tok(0), pl.BlockSpec((None, hpb, HEAD_DIM, HEAD_DIM), lambda b, h, t: (b, h, 0, 0))],
        out_shape=[jax.ShapeDtypeStruct((bsz, length, MIX_W), BF16),
                   jax.ShapeDtypeStruct((bsz, HEADS, HEAD_DIM, HEAD_DIM), F32)],
        scratch_shapes=[pltpu.VMEM((hpb, HEAD_DIM, HEAD_DIM), F32)]
        + [pltpu.VMEM((hpb, tt + SUB, HEAD_DIM), F32)] * 3,
        compiler_params=pltpu.CompilerParams(
            dimension_semantics=("parallel", "parallel", "arbitrary"), vmem_limit_bytes=VMEM_LIMIT),
        name="hgrn2",
    )(q_og, lf, plain, q_og, gain, st0, tri, lvl)


def _sb_kernel(q_ref, k_ref, v_ref, kp_ref, vp_ref, tri_ref, o_ref, acc_ref, rem_ref, *, ts, n_sub):
    qi = pl.program_id(2)
    scale = 1.0 / math.sqrt(HEAD_DIM)
    r_i = lax.broadcasted_iota(jnp.int32, (ts, ts), 0)
    c_i = lax.broadcasted_iota(jnp.int32, (ts, ts), 1)
    causal = c_i < r_i
    prefix_valid = jnp.logical_and(c_i >= PAD, c_i < BLOCK)

    def tile(j):
        start = pl.multiple_of(j * ts, ts)
        return k_ref[pl.ds(start, ts), :], v_ref[pl.ds(start, ts), :]

    def log_weights(q, k_t, mask):
        z = _dot_nt(q, k_t) * scale
        log_beta = jnp.minimum(z, 0.0) - jnp.log(1.0 + jnp.exp(-jnp.abs(z)))
        log_rest = log_beta - z
        if mask is not None:
            log_rest = jnp.where(mask, log_rest, 0.0)
        hi, lo = _split_bf16(log_rest)
        later = _dot(jnp.concatenate([hi, lo], axis=1), tri_ref[...])
        return log_beta + later, jnp.sum(log_rest, axis=-1, keepdims=True)

    def weights(log_w, mask):
        w = jnp.exp(log_w)
        if mask is not None:
            w = jnp.where(mask, w, 0.0)
        return w.astype(BF16)

    def first_visits(sub0_has_prev):
        for s in range(n_sub):
            q = q_ref[s * ts:(s + 1) * ts, :]
            g = qi * n_sub + s
            k_t, v_t = tile(g)
            log_w, rem = log_weights(q, k_t, causal)
            acc = _dot(weights(log_w, causal), v_t)
            if s > 0 or sub0_has_prev:
                k_t, v_t = tile(g - 1)
                log_w, total = log_weights(q, k_t, None)
                acc = acc + _dot(weights(log_w + rem, None), v_t)
                rem = rem + total
            acc_ref[s] = acc
            rem_ref[s] = rem

    @pl.when(qi > 0)
    def _():
        first_visits(True)

    @pl.when(qi == 0)
    def _():
        first_visits(False)

    for s in range(n_sub):
        q = q_ref[s * ts:(s + 1) * ts, :]

        def visit(k_t, v_t, mask):
            log_w, total = log_weights(q, k_t, mask)
            rem = rem_ref[s]
            acc_ref[s] += _dot(weights(log_w + rem, mask), v_t)
            rem = rem + total
            rem_ref[s] = rem
            return jnp.max(rem)

        def cond(state):
            j, live = state
            return jnp.logical_and(j >= 0, live > SB_DEAD_LOG)

        def body(state):
            j, _ = state
            return j - 1, visit(*tile(j), None)

        _, live = lax.while_loop(cond, body, (qi * n_sub + s - 2, jnp.max(rem_ref[s])))

        @pl.when(live > SB_DEAD_LOG)
        def _():
            visit(kp_ref[...], vp_ref[...], prefix_valid)

        o_ref[s * ts:(s + 1) * ts, :] = acc_ref[s].astype(BF16)


def _stick_breaking(plain, kp, vp, *, n_sub):
    bsz, length, _ = plain.shape
    ts = SB_TILE
    tq = ts * n_sub
    assert length % tq == 0
    r = lax.broadcasted_iota(jnp.int32, (2 * ts, ts), 0) % ts
    c = lax.broadcasted_iota(jnp.int32, (2 * ts, ts), 1)
    tri = (r > c).astype(BF16)
    seq = lambda off: pl.BlockSpec((None, length, HEAD_DIM), lambda b, h, i: (b, 0, h + off))
    pre = pl.BlockSpec((ts, HEAD_DIM), lambda b, h, i: (0, h))
    return pl.pallas_call(
        functools.partial(_sb_kernel, ts=ts, n_sub=n_sub),
        grid=(bsz, HEADS, length // tq),
        in_specs=[pl.BlockSpec((None, tq, HEAD_DIM), lambda b, h, i: (b, i, h + HEADS)),
                  seq(2 * HEADS), seq(3 * HEADS), pre, pre,
                  pl.BlockSpec((2 * ts, ts), lambda b, h, i: (0, 0))],
        out_specs=pl.BlockSpec((None, tq, HEAD_DIM), lambda b, h, i: (b, i, h)),
        out_shape=jax.ShapeDtypeStruct((bsz, length, MIX_W), BF16),
        scratch_shapes=[pltpu.VMEM((n_sub, ts, HEAD_DIM), F32), pltpu.VMEM((n_sub, ts, 1), F32)],
        compiler_params=pltpu.CompilerParams(
            dimension_semantics=("parallel", "parallel", "arbitrary"), vmem_limit_bytes=VMEM_LIMIT),
        name="stick_breaking",
    )(plain, plain, plain, kp, vp, tri)


def _merge_ln_kernel(h_ref, ohg_ref, osb_ref, ghg_ref, gsb_ref, php_ref, psb_ref, wo_ref, g_ref, b_ref, o_ref):
    y = (ghg_ref[...].astype(F32) * _dot(ohg_ref[...], php_ref[...])
         + gsb_ref[...].astype(F32) * _dot(osb_ref[...], psb_ref[...]))
    mix = _dot(y.astype(BF16), wo_ref[...])
    o_ref[...] = _layer_norm(DN_ALPHA * h_ref[...] + mix, g_ref[...], b_ref[...])


def _merge_ln(h, o_hg, o_sb, gates, p_hg, p_sb, w_out, g, b, *, tm):
    n, d = h.shape
    assert n % tm == 0
    row = lambda w: pl.BlockSpec((tm, w), lambda i: (i, 0))
    full = lambda a: pl.BlockSpec(a.shape, lambda i: (0, 0), pipeline_mode=pl.Buffered(1))
    return pl.pallas_call(
        _merge_ln_kernel,
        grid=(n // tm,),
        in_specs=[row(d), row(MIX_W), row(MIX_W),
                  pl.BlockSpec((tm, d), lambda i: (i, 0)), pl.BlockSpec((tm, d), lambda i: (i, 1)),
                  full(p_hg), full(p_sb), full(w_out), full(g), full(b)],
        out_specs=row(d),
        out_shape=jax.ShapeDtypeStruct((n, d), F32),
        compiler_params=pltpu.CompilerParams(
            dimension_semantics=("parallel",), vmem_limit_bytes=VMEM_LIMIT),
        name="merge_ln",
    )(h, o_hg, o_sb, gates, gates, p_hg, p_sb, w_out, g, b)


def _pick(n, pref):
    t = min(n, pref)
    while n % t:
        t //= 2
    return t


def kernel(x, meta, ln1_g, ln1_b, ffn1_w_gate, ffn1_w_up, ffn1_w_down, w_in, b_gate, hg_lb_logits, hg_norm_g, w_proj_hg, w_proj_sb, w_out, ln2_g, ln2_b, ffn2_w_gate, ffn2_w_up, ffn2_w_down, ln3_g, ln3_b):
    bsz, seq, d = x.shape
    assert ln1_g.shape[0] == DEPTH and hg_lb_logits.shape == (DEPTH + 1, MIX_W) and (2 * d) % MIX_W == 0
    n = bsz * seq
    tm = _pick(n, 1024)
    tf = _pick(ffn1_w_gate.shape[2], 512)
    bf = lambda a: a[0].astype(BF16)
    row = lambda a: a[0].reshape(1, -1)

    w1 = (bf(ffn1_w_gate), bf(ffn1_w_up), bf(ffn1_w_down), row(ln1_g), row(ln1_b))
    w2 = (bf(ffn2_w_gate), bf(ffn2_w_up), bf(ffn2_w_down), row(ln3_g), row(ln3_b))
    w_in_b = bf(w_in)
    bg = row(b_gate)
    gain = row(hg_norm_g)

    hp = jnp.concatenate([jnp.zeros((PAD, d), x.dtype), meta.astype(x.dtype)], axis=0)
    _, hpb = _ffn_ln(hp, *w1, tm=BLOCK, tf=tf, emit_bf16=True)
    _, lf_p, plain_p, _ = _proj_all(hpb, w_in_b, bg, hg_lb_logits, tm=BLOCK, n_inert=PAD)
    _, st0 = _hgrn(jnp.zeros((1, BLOCK, 2 * MIX_W), BF16), lf_p[None], plain_p[None], gain,
                   jnp.zeros((HEADS, HEAD_DIM, HEAD_DIM), F32), tt=BLOCK, hpb=HG_HEADS_PER_STEP)
    pad_rows = ((0, SB_TILE - BLOCK), (0, 0))
    kp = jnp.pad(plain_p[:, 2 * MIX_W:3 * MIX_W], pad_rows)
    vp = jnp.pad(plain_p[:, 3 * MIX_W:], pad_rows)

    h, hb = _ffn_ln(x.reshape(n, d), *w1, tm=tm, tf=tf, emit_bf16=True)
    q_og, lf, plain, gates = _proj_all(hb, w_in_b, bg, hg_lb_logits, tm=tm, n_inert=0)
    seq3 = lambda a: a.reshape(bsz, seq, a.shape[-1])
    o_hg, _ = _hgrn(seq3(q_og), seq3(lf), seq3(plain), gain, st0[0], tt=_pick(seq, 256),
                    hpb=HG_HEADS_PER_STEP)
    o_sb = _stick_breaking(seq3(plain), kp, vp, n_sub=_pick(seq // SB_TILE, 4))
    h = _merge_ln(h, o_hg.reshape(n, MIX_W), o_sb.reshape(n, MIX_W), gates, bf(w_proj_hg), bf(w_proj_sb),
                  bf(w_out), row(ln2_g), row(ln2_b), tm=_pick(n, 512))
    (h,) = _ffn_ln(h, *w2, tm=tm, tf=tf, emit_bf16=False)
    return h.reshape(bsz, seq, d)
```

```python
import functools
import math

import jax
import jax.numpy as jnp
from jax import lax
from jax.experimental import pallas as pl
from jax.experimental.pallas import tpu as pltpu

N_META = 16
BLOCK = 128
PAD = BLOCK - N_META
HEADS = 8
HEAD_DIM = 128
MIX_W = HEADS * HEAD_DIM
SUB = 8
HG_HEADS_PER_STEP = 2
HG_LEVEL_TILE = 128
LOG2_E = math.log2(math.e)
SB_TILE = 256
LN_EPS = 1e-5
RMS_EPS = 1e-6
DEPTH = 1
DN_ALPHA = (2.0 * DEPTH) ** 0.25
SB_DEAD_LOG = -104.0
MXU_COLS = 256
VMEM_LIMIT = 60 * 1024 * 1024

F32 = jnp.float32
BF16 = jnp.bfloat16


def _layer_norm(y, g, b):
    mu = jnp.mean(y, axis=-1, keepdims=True)
    yc = y - mu
    var = jnp.mean(yc * yc, axis=-1, keepdims=True)
    return yc * lax.rsqrt(var + LN_EPS) * g + b


def _dot(a, b):
    return jnp.dot(a, b, preferred_element_type=F32)


def _dot_nt(a, b):
    return lax.dot_general(a, b, (((1,), (1,)), ((), ())), preferred_element_type=F32)


def _dot_tn(a, b):
    return lax.dot_general(a, b, (((0,), (0,)), ((), ())), preferred_element_type=F32)


def _split_bf16(x):
    hi = x.astype(BF16)
    return hi, (x - hi.astype(F32)).astype(BF16)


def _ffn_ln_kernel(x_hbm, wg_ref, wu_ref, wd_ref, g_ref, b_ref, o_ref, *rest, emit_bf16, tm):
    xb_ref, xbuf_ref, sem = rest
    ob_ref = xb_ref if emit_bf16 else None
    i = pl.program_id(0)
    j = pl.program_id(1)
    n_i = pl.num_programs(0)
    n_j = pl.num_programs(1)

    def x_copy(tile):
        return pltpu.make_async_copy(x_hbm.at[pl.ds(pl.multiple_of(tile * tm, tm), tm), :], xbuf_ref, sem)

    @pl.when(jnp.logical_and(i == 0, j == 0))
    def _():
        x_copy(0).start()

    @pl.when(j == 0)
    def _():
        x_copy(i).wait()
        x = xbuf_ref[...]
        xb_ref[...] = x.astype(BF16)
        o_ref[...] = (DN_ALPHA / 0.5) * x

    @pl.when(jnp.logical_and(j == jnp.minimum(1, n_j - 1), i + 1 < n_i))
    def _():
        x_copy(i + 1).start()

    xb = xb_ref[...]
    gate = _dot(xb, wg_ref[...])
    up = _dot(xb, wu_ref[...])
    act = (gate * jax.nn.sigmoid(gate) * up).astype(BF16)
    o_ref[...] += _dot(act, wd_ref[...])

    @pl.when(j == n_j - 1)
    def _():
        y = _layer_norm(0.5 * o_ref[...], g_ref[...], b_ref[...])
        o_ref[...] = y
        if ob_ref is not None:
            ob_ref[...] = y.astype(BF16)


def _ffn_ln(x, wg, wu, wd, g, b, *, tm, tf, emit_bf16):
    n, d = x.shape
    dff = wg.shape[1]
    assert n % tm == 0 and dff % tf == 0
    rows = pl.BlockSpec((tm, d), lambda i, j: (i, 0))
    out_shape = [jax.ShapeDtypeStruct((n, d), F32)]
    if emit_bf16:
        out_shape.append(jax.ShapeDtypeStruct((n, d), BF16))
    return pl.pallas_call(
        functools.partial(_ffn_ln_kernel, emit_bf16=emit_bf16, tm=tm),
        grid=(n // tm, dff // tf),
        in_specs=[
            pl.BlockSpec(memory_space=pl.ANY),
            pl.BlockSpec((d, tf), lambda i, j: (0, j)),
            pl.BlockSpec((d, tf), lambda i, j: (0, j)),
            pl.BlockSpec((tf, d), lambda i, j: (j, 0)),
            pl.BlockSpec((1, d), lambda i, j: (0, 0)),
            pl.BlockSpec((1, d), lambda i, j: (0, 0)),
        ],
        out_specs=[rows] * len(out_shape),
        out_shape=out_shape,
        scratch_shapes=([] if emit_bf16 else [pltpu.VMEM((tm, d), BF16)])
        + [pltpu.VMEM((tm, d), F32), pltpu.SemaphoreType.DMA(())],
        compiler_params=pltpu.CompilerParams(
            dimension_semantics=("arbitrary", "arbitrary"), vmem_limit_bytes=VMEM_LIMIT),
        name="ffn_ln",
    )(x, wg, wu, wd, g, b)


def _proj_kernel(x_ref, w_ref, *rest, act, n_inert, tm):
    o_ref = rest[-1]
    x = x_ref[...]
    for c in range(MIX_W // MXU_COLS):
        cols = slice(c * MXU_COLS, (c + 1) * MXU_COLS)
        acc = _dot(x, w_ref[:, cols])
        if act == "silu":
            y = acc * jax.nn.sigmoid(acc)
        elif act == "gate":
            y = jax.nn.sigmoid(acc + rest[0][:, cols])
        elif act == "log_forget":
            l0 = rest[0][0:1, cols]
            l1 = rest[0][1:2, cols]
            m = jnp.maximum(l0, l1)
            e0 = jnp.exp(l0 - m)
            e1 = jnp.exp(l1 - m)
            lb = e0 / (e0 + e1)
            y = jnp.log(lb + (1.0 - lb) * jax.nn.sigmoid(acc))
            if n_inert:
                row = pl.program_id(0) * tm + lax.broadcasted_iota(jnp.int32, y.shape, 0)
                y = jnp.where(row >= n_inert, y, 0.0)
        else:
            y = acc
        o_ref[:, cols] = y.astype(o_ref.dtype)


def _proj(xb, w_in, extra, *, act, col_block, n_blocks, out_dtype, tm, n_inert=0):
    n, d = xb.shape
    assert n % tm == 0
    in_specs = [pl.BlockSpec((tm, d), lambda i, j: (i, 0)),
                pl.BlockSpec((d, MIX_W), lambda i, j: (0, col_block(j)))]
    args = [xb, w_in]
    if act == "gate":
        in_specs.append(pl.BlockSpec((1, MIX_W), lambda i, j: (0, j)))
        args.append(extra)
    elif act == "log_forget":
        in_specs.append(pl.BlockSpec(extra.shape, lambda i, j: (0, 0)))
        args.append(extra)
    return pl.pallas_call(
        functools.partial(_proj_kernel, act=act, n_inert=n_inert, tm=tm),
        grid=(n // tm, n_blocks),
        in_specs=in_specs,
        out_specs=pl.BlockSpec((tm, MIX_W), lambda i, j: (i, j)),
        out_shape=jax.ShapeDtypeStruct((n, n_blocks * MIX_W), out_dtype),
        compiler_params=pltpu.CompilerParams(
            dimension_semantics=("parallel", "arbitrary"), vmem_limit_bytes=VMEM_LIMIT),
        name="proj_" + act,
    )(*args)


def _proj_all(xb, w_in, b_gate, lb_logits, *, tm, n_inert):
    d = xb.shape[1]
    silu = _proj(xb, w_in, None, act="silu", col_block=lambda j: 3 * j, n_blocks=2, out_dtype=BF16, tm=tm)
    log_f = _proj(xb, w_in, lb_logits, act="log_forget", col_block=lambda j: 1, n_blocks=1, out_dtype=F32,
                  tm=tm, n_inert=n_inert)
    plain = _proj(xb, w_in, None, act="none", col_block=lambda j: jnp.where(j > 0, j + 3, 2), n_blocks=4,
                  out_dtype=BF16, tm=tm)
    gates = _proj(xb, w_in, b_gate, act="gate", col_block=lambda j: j + 7, n_blocks=2 * d // MIX_W,
                  out_dtype=BF16, tm=tm)
    return silu, log_f, plain, gates


def _hgrn_levels(tt):
    return [m for m in (16, 32, 64, 128, 256, 512) if m <= tt]


def _hgrn_kernel(q_ref, lf_ref, v_ref, og_ref, gain_ref, st0_ref, tri_ref, lvl_ref, o_ref, stf_ref, st_ref,
                 fpad_ref, kpad_ref, vpad_ref, *, tt, lt, hpb):
    t_id = pl.program_id(2)

    @pl.when(t_id == 0)
    def _():
        st_ref[...] = st0_ref[...]

    row = lax.broadcasted_iota(jnp.int32, (tt, HEAD_DIM), 0)
    sub_start = (row & (SUB - 1)) == 0
    zeros = jnp.zeros((SUB, HEAD_DIM), F32)
    lvl = lvl_ref[...]

    for hd in range(hpb):
        lanes = slice(hd * HEAD_DIM, (hd + 1) * HEAD_DIM)
        qb = q_ref[:, lanes]
        q = qb.astype(F32)
        g2 = lf_ref[:, lanes] * LOG2_E
        vb = v_ref[:, lanes]
        v = vb.astype(F32)
        f = jnp.exp2(g2)
        kk = 1.0 - f
        kb = kk.astype(BF16)
        g_hi, g_lo = _split_bf16(g2)

        fpad_ref[hd, 0:SUB, :] = zeros
        kpad_ref[hd, 0:SUB, :] = zeros
        vpad_ref[hd, 0:SUB, :] = zeros
        fpad_ref[hd, SUB:, :] = jnp.where(sub_start, 0.0, f)
        kpad_ref[hd, SUB:, :] = kk
        vpad_ref[hd, SUB:, :] = v
        o = jnp.sum(q * kk, axis=-1, keepdims=True) * v
        u = q
        for d in range(1, SUB):
            u = u * fpad_ref[hd, pl.ds(SUB - d + 1, tt), :]
            k_d = kpad_ref[hd, pl.ds(SUB - d, tt), :]
            v_d = vpad_ref[hd, pl.ds(SUB - d, tt), :]
            o = o + jnp.sum(u * k_d, axis=-1, keepdims=True) * v_d

        for s in range(tt // lt):
            rows = slice(s * lt, (s + 1) * lt)
            b = _dot(tri_ref[...], jnp.concatenate([g_hi[rows], g_lo[rows]], axis=0))

            scores = jnp.zeros((lt, lt), F32)
            for idx, m in enumerate(_hgrn_levels(lt)):
                mid = b.reshape(lt // m, m, HEAD_DIM)[:, m // 2 - 1:m // 2, :]
                b_mid = jnp.broadcast_to(mid, (lt // m, m, HEAD_DIM)).reshape(lt, HEAD_DIM)
                e = jnp.exp2(-jnp.abs(b - b_mid)).astype(BF16)
                scores = jnp.where(lvl == idx, _dot_nt(qb[rows] * e, kb[rows] * e), scores)
            st = st_ref[hd]
            o_s = (o[rows] + _dot(scores.astype(BF16), vb[rows])
                   + _dot_nt(qb[rows] * jnp.exp2(b).astype(BF16), st.astype(BF16)))

            b_end = b[lt - 1:lt, :]
            k_end = kb[rows] * jnp.exp2(b_end - b).astype(BF16)
            st_ref[hd] = st * jnp.exp2(b_end) + _dot_tn(vb[rows], k_end)

            o_s = o_s * lax.rsqrt(jnp.mean(o_s * o_s, axis=-1, keepdims=True) + RMS_EPS)
            o_ref[rows, lanes] = (o_s * gain_ref[:, lanes] * og_ref[rows, lanes].astype(F32)).astype(BF16)

    @pl.when(t_id == pl.num_programs(2) - 1)
    def _():
        stf_ref[...] = st_ref[...]


def _hgrn(q_og, lf, plain, gain, st0, *, tt, hpb):
    bsz, length, _ = lf.shape
    lt = min(tt, HG_LEVEL_TILE)
    levels = _hgrn_levels(lt)
    assert length % tt == 0 and tt % lt == 0 and levels[-1] == lt and SUB * 2 == levels[0] and HEADS % hpb == 0
    r = lax.broadcasted_iota(jnp.int32, (lt, 2 * lt), 0)
    c = lax.broadcasted_iota(jnp.int32, (lt, 2 * lt), 1) % lt
    tri = (c <= r).astype(BF16)
    r = lax.broadcasted_iota(jnp.int32, (lt, lt), 0)
    c = lax.broadcasted_iota(jnp.int32, (lt, lt), 1)
    lvl = jnp.full((lt, lt), -1, jnp.int32)
    for idx, m in reversed(list(enumerate(levels))):
        lvl = jnp.where(r // m == c // m, idx, lvl)
    lvl = jnp.where(jnp.logical_or(r // SUB == c // SUB, c > r), -1, lvl)
    groups = HEADS // hpb
    width = hpb * HEAD_DIM
    tok = lambda off: pl.BlockSpec((None, tt, width), lambda b, h, t: (b, t, h + off))
    const = lambda a: pl.BlockSpec(a.shape, lambda b, h, t: (0, 0))
    return pl.pallas_call(
        functools.partial(_hgrn_kernel, tt=tt, lt=lt, hpb=hpb),
        grid=(bsz, groups, length // tt),
        in_specs=[tok(0), tok(0), tok(0), tok(groups),
                  pl.BlockSpec((1, width), lambda b, h, t: (0, h)),
                  pl.BlockSpec((hpb, HEAD_DIM, HEAD_DIM), lambda b, h, t: (h, 0, 0)),
                  const(tri), const(lvl)],
        out_specs=[tok(0), pl.BlockSpec((None, hpb, HEAD_DIM, HEAD_DIM), lambda b, h, t: (b, h, 0, 0))],
        out_shape=[jax.ShapeDtypeStruct((bsz, length, MIX_W), BF16),
                   jax.ShapeDtypeStruct((bsz, HEADS, HEAD_DIM, HEAD_DIM), F32)],
        scratch_shapes=[pltpu.VMEM((hpb, HEAD_DIM, HEAD_DIM), F32)]
        + [pltpu.VMEM((hpb, tt + SUB, HEAD_DIM), F32)] * 3,
        compiler_params=pltpu.CompilerParams(
            dimension_semantics=("parallel", "parallel", "arbitrary"), vmem_limit_bytes=VMEM_LIMIT),
        name="hgrn2",
    )(q_og, lf, plain, q_og, gain, st0, tri, lvl)


def _sb_kernel(q_ref, k_ref, v_ref, kp_ref, vp_ref, tri_ref, o_ref, acc_ref, rem_ref, *, ts, n_sub):
    qi = pl.program_id(2)
    r_i = lax.broadcasted_iota(jnp.int32, (ts, ts), 0)
    c_i = lax.broadcasted_iota(jnp.int32, (ts, ts), 1)
    causal = c_i < r_i
    prefix_valid = jnp.logical_and(c_i >= PAD, c_i < BLOCK)

    def tile(j):
        start = pl.multiple_of(j * ts, ts)
        return k_ref[pl.ds(start, ts), :], v_ref[pl.ds(start, ts), :]

    def log_weights(q, k_t, mask):
        z = _dot_nt(q, k_t)
        log_beta = jnp.minimum(z, 0.0) - jnp.log(1.0 + jnp.exp(-jnp.abs(z)))
        log_rest = log_beta - z
        if mask is not None:
            log_rest = jnp.where(mask, log_rest, 0.0)
        hi, lo = _split_bf16(log_rest)
        later = _dot(jnp.concatenate([hi, lo], axis=1), tri_ref[...])
        return log_beta + later, jnp.sum(log_rest, axis=-1, keepdims=True)

    def weights(log_w, mask):
        w = jnp.exp(log_w)
        if mask is not None:
            w = jnp.where(mask, w, 0.0)
        return w.astype(BF16)

    def first_visits(sub0_has_prev):
        for s in range(n_sub):
            q = q_ref[s * ts:(s + 1) * ts, :]
            g = qi * n_sub + s
            k_t, v_t = tile(g)
            log_w, rem = log_weights(q, k_t, causal)
            acc = _dot(weights(log_w, causal), v_t)
            if s > 0 or sub0_has_prev:
                k_t, v_t = tile(g - 1)
                log_w, total = log_weights(q, k_t, None)
                acc = acc + _dot(weights(log_w + rem, None), v_t)
                rem = rem + total
            acc_ref[s] = acc
            rem_ref[s] = rem

    @pl.when(qi > 0)
    def _():
        first_visits(True)

    @pl.when(qi == 0)
    def _():
        first_visits(False)

    for s in range(n_sub):
        q = q_ref[s * ts:(s + 1) * ts, :]

        def visit(k_t, v_t, mask):
            log_w, total = log_weights(q, k_t, mask)
            rem = rem_ref[s]
            acc_ref[s] += _dot(weights(log_w + rem, mask), v_t)
            rem = rem + total
            rem_ref[s] = rem
            return jnp.max(rem)

        def cond(state):
            j, live = state
            return jnp.logical_and(j >= 0, live > SB_DEAD_LOG)

        def body(state):
            j, _ = state
            return j - 1, visit(*tile(j), None)

        _, live = lax.while_loop(cond, body, (qi * n_sub + s - 2, jnp.max(rem_ref[s])))

        @pl.when(live > SB_DEAD_LOG)
        def _():
            visit(kp_ref[...], vp_ref[...], prefix_valid)

        o_ref[s * ts:(s + 1) * ts, :] = acc_ref[s].astype(BF16)


def _stick_breaking(plain, kp, vp, *, n_sub):
    bsz, length, _ = plain.shape
    ts = SB_TILE
    tq = ts * n_sub
    assert length % tq == 0
    r = lax.broadcasted_iota(jnp.int32, (2 * ts, ts), 0) % ts
    c = lax.broadcasted_iota(jnp.int32, (2 * ts, ts), 1)
    tri = (r > c).astype(BF16)
    seq = lambda off: pl.BlockSpec((None, length, HEAD_DIM), lambda b, h, i: (b, 0, h + off))
    pre = pl.BlockSpec((ts, HEAD_DIM), lambda b, h, i: (0, h))
    return pl.pallas_call(
        functools.partial(_sb_kernel, ts=ts, n_sub=n_sub),
        grid=(bsz, HEADS, length // tq),
        in_specs=[pl.BlockSpec((None, tq, HEAD_DIM), lambda b, h, i: (b, i, h + HEADS)),
                  seq(2 * HEADS), seq(3 * HEADS), pre, pre,
                  pl.BlockSpec((2 * ts, ts), lambda b, h, i: (0, 0))],
        out_specs=pl.BlockSpec((None, tq, HEAD_DIM), lambda b, h, i: (b, i, h)),
        out_shape=jax.ShapeDtypeStruct((bsz, length, MIX_W), BF16),
        scratch_shapes=[pltpu.VMEM((n_sub, ts, HEAD_DIM), F32), pltpu.VMEM((n_sub, ts, 1), F32)],
        compiler_params=pltpu.CompilerParams(
            dimension_semantics=("parallel", "parallel", "arbitrary"), vmem_limit_bytes=VMEM_LIMIT),
        name="stick_breaking",
    )(plain, plain, plain, kp, vp, tri)


def _merge_ln_kernel(h_ref, ohg_ref, osb_ref, ghg_ref, gsb_ref, php_ref, psb_ref, wo_ref, g_ref, b_ref, o_ref):
    y = (ghg_ref[...].astype(F32) * _dot(ohg_ref[...], php_ref[...])
         + gsb_ref[...].astype(F32) * _dot(osb_ref[...], psb_ref[...]))
    mix = _dot(y.astype(BF16), wo_ref[...])
    o_ref[...] = _layer_norm(DN_ALPHA * h_ref[...] + mix, g_ref[...], b_ref[...])


def _merge_ln(h, o_hg, o_sb, gates, p_hg, p_sb, w_out, g, b, *, tm):
    n, d = h.shape
    assert n % tm == 0
    row = lambda w: pl.BlockSpec((tm, w), lambda i: (i, 0))
    full = lambda a: pl.BlockSpec(a.shape, lambda i: (0, 0), pipeline_mode=pl.Buffered(1))
    return pl.pallas_call(
        _merge_ln_kernel,
        grid=(n // tm,),
        in_specs=[row(d), row(MIX_W), row(MIX_W),
                  pl.BlockSpec((tm, d), lambda i: (i, 0)), pl.BlockSpec((tm, d), lambda i: (i, 1)),
                  full(p_hg), full(p_sb), full(w_out), full(g), full(b)],
        out_specs=row(d),
        out_shape=jax.ShapeDtypeStruct((n, d), F32),
        compiler_params=pltpu.CompilerParams(
            dimension_semantics=("parallel",), vmem_limit_bytes=VMEM_LIMIT),
        name="merge_ln",
    )(h, o_hg, o_sb, gates, gates, p_hg, p_sb, w_out, g, b)


def _pick(n, pref):
    t = min(n, pref)
    while n % t:
        t //= 2
    return t


def kernel(x, meta, ln1_g, ln1_b, ffn1_w_gate, ffn1_w_up, ffn1_w_down, w_in, b_gate, hg_lb_logits, hg_norm_g, w_proj_hg, w_proj_sb, w_out, ln2_g, ln2_b, ffn2_w_gate, ffn2_w_up, ffn2_w_down, ln3_g, ln3_b):
    bsz, seq, d = x.shape
    assert ln1_g.shape[0] == DEPTH and hg_lb_logits.shape == (DEPTH + 1, MIX_W) and (2 * d) % MIX_W == 0
    n = bsz * seq
    tm = _pick(n, 1024)
    tf = _pick(ffn1_w_gate.shape[2], 512)
    bf = lambda a: a[0].astype(BF16)
    row = lambda a: a[0].reshape(1, -1)

    w1 = (bf(ffn1_w_gate), bf(ffn1_w_up), bf(ffn1_w_down), row(ln1_g), row(ln1_b))
    w2 = (bf(ffn2_w_gate), bf(ffn2_w_up), bf(ffn2_w_down), row(ln3_g), row(ln3_b))
    col = lax.broadcasted_iota(jnp.int32, (1, w_in.shape[2]), 1)
    is_sq = jnp.logical_and(col >= 4 * MIX_W, col < 5 * MIX_W)
    w_in_b = (w_in[0] * jnp.where(is_sq, 1.0 / math.sqrt(HEAD_DIM), 1.0)).astype(BF16)
    bg = row(b_gate)
    gain = row(hg_norm_g)

    hp = jnp.concatenate([jnp.zeros((PAD, d), x.dtype), meta.astype(x.dtype)], axis=0)
    _, hpb = _ffn_ln(hp, *w1, tm=BLOCK, tf=tf, emit_bf16=True)
    _, lf_p, plain_p, _ = _proj_all(hpb, w_in_b, bg, hg_lb_logits, tm=BLOCK, n_inert=PAD)
    _, st0 = _hgrn(jnp.zeros((1, BLOCK, 2 * MIX_W), BF16), lf_p[None], plain_p[None], gain,
                   jnp.zeros((HEADS, HEAD_DIM, HEAD_DIM), F32), tt=BLOCK, hpb=HG_HEADS_PER_STEP)
    pad_rows = ((0, SB_TILE - BLOCK), (0, 0))
    kp = jnp.pad(plain_p[:, 2 * MIX_W:3 * MIX_W], pad_rows)
    vp = jnp.pad(plain_p[:, 3 * MIX_W:], pad_rows)

    h, hb = _ffn_ln(x.reshape(n, d), *w1, tm=tm, tf=tf, emit_bf16=True)
    q_og, lf, plain, gates = _proj_all(hb, w_in_b, bg, hg_lb_logits, tm=tm, n_inert=0)
    seq3 = lambda a: a.reshape(bsz, seq, a.shape[-1])
    o_hg, _ = _hgrn(seq3(q_og), seq3(lf), seq3(plain), gain, st0[0], tt=_pick(seq, 256),
                    hpb=HG_HEADS_PER_STEP)
    o_sb = _stick_breaking(seq3(plain), kp, vp, n_sub=_pick(seq // SB_TILE, 4))
    h = _merge_ln(h, o_hg.reshape(n, MIX_W), o_sb.reshape(n, MIX_W), gates, bf(w_proj_hg), bf(w_proj_sb),
                  bf(w_out), row(ln2_g), row(ln2_b), tm=_pick(n, 512))
    (h,) = _ffn_ln(h, *w2, tm=tm, tf=tf, emit_bf16=False)
    return h.reshape(bsz, seq, d)
```

```python
import functools
import math

import jax
import jax.numpy as jnp
from jax import lax
from jax.experimental import pallas as pl
from jax.experimental.pallas import tpu as pltpu

N_META = 16
BLOCK = 128
PAD = BLOCK - N_META
HEADS = 8
HEAD_DIM = 128
MIX_W = HEADS * HEAD_DIM
SUB = 8
HG_HEADS_PER_STEP = 2
HG_LEVEL_TILE = 128
LOG2_E = math.log2(math.e)
SB_TILE = 256
LN_EPS = 1e-5
RMS_EPS = 1e-6
DEPTH = 1
DN_ALPHA = (2.0 * DEPTH) ** 0.25
SB_DEAD_LOG = -104.0
MXU_COLS = 256
VMEM_LIMIT = 60 * 1024 * 1024

F32 = jnp.float32
BF16 = jnp.bfloat16


def _layer_norm(y, g, b):
    mu = jnp.mean(y, axis=-1, keepdims=True)
    yc = y - mu
    var = jnp.mean(yc * yc, axis=-1, keepdims=True)
    return yc * lax.rsqrt(var + LN_EPS) * g + b


def _dot(a, b):
    return jnp.dot(a, b, preferred_element_type=F32)


def _dot_nt(a, b):
    return lax.dot_general(a, b, (((1,), (1,)), ((), ())), preferred_element_type=F32)


def _dot_tn(a, b):
    return lax.dot_general(a, b, (((0,), (0,)), ((), ())), preferred_element_type=F32)


def _split_bf16(x):
    hi = x.astype(BF16)
    return hi, (x - hi.astype(F32)).astype(BF16)


def _ffn_ln_kernel(x_hbm, wg_ref, wu_ref, wd_ref, g_ref, b_ref, o_ref, *rest, emit_bf16, tm):
    xb_ref, xbuf_ref, sem = rest
    ob_ref = xb_ref if emit_bf16 else None
    i = pl.program_id(0)
    j = pl.program_id(1)
    n_i = pl.num_programs(0)
    n_j = pl.num_programs(1)

    def x_copy(tile):
        return pltpu.make_async_copy(x_hbm.at[pl.ds(pl.multiple_of(tile * tm, tm), tm), :], xbuf_ref, sem)

    @pl.when(jnp.logical_and(i == 0, j == 0))
    def _():
        x_copy(0).start()

    @pl.when(j == 0)
    def _():
        x_copy(i).wait()
        x = xbuf_ref[...]
        xb_ref[...] = x.astype(BF16)
        o_ref[...] = (DN_ALPHA / 0.5) * x

    @pl.when(jnp.logical_and(j == jnp.minimum(1, n_j - 1), i + 1 < n_i))
    def _():
        x_copy(i + 1).start()

    xb = xb_ref[...]
    gate = _dot(xb, wg_ref[...])
    up = _dot(xb, wu_ref[...])
    act = (gate * jax.nn.sigmoid(gate) * up).astype(BF16)
    o_ref[...] += _dot(act, wd_ref[...])

    @pl.when(j == n_j - 1)
    def _():
        y = _layer_norm(0.5 * o_ref[...], g_ref[...], b_ref[...])
        o_ref[...] = y
        if ob_ref is not None:
            ob_ref[...] = y.astype(BF16)


def _ffn_ln(x, wg, wu, wd, g, b, *, tm, tf, emit_bf16):
    n, d = x.shape
    dff = wg.shape[1]
    assert n % tm == 0 and dff % tf == 0
    rows = pl.BlockSpec((tm, d), lambda i, j: (i, 0))
    out_shape = [jax.ShapeDtypeStruct((n, d), F32)]
    if emit_bf16:
        out_shape.append(jax.ShapeDtypeStruct((n, d), BF16))
    return pl.pallas_call(
        functools.partial(_ffn_ln_kernel, emit_bf16=emit_bf16, tm=tm),
        grid=(n // tm, dff // tf),
        in_specs=[
            pl.BlockSpec(memory_space=pl.ANY),
            pl.BlockSpec((d, tf), lambda i, j: (0, j)),
            pl.BlockSpec((d, tf), lambda i, j: (0, j)),
            pl.BlockSpec((tf, d), lambda i, j: (j, 0)),
            pl.BlockSpec((1, d), lambda i, j: (0, 0)),
            pl.BlockSpec((1, d), lambda i, j: (0, 0)),
        ],
        out_specs=[rows] * len(out_shape),
        out_shape=out_shape,
        scratch_shapes=([] if emit_bf16 else [pltpu.VMEM((tm, d), BF16)])
        + [pltpu.VMEM((tm, d), F32), pltpu.SemaphoreType.DMA(())],
        compiler_params=pltpu.CompilerParams(
            dimension_semantics=("arbitrary", "arbitrary"), vmem_limit_bytes=VMEM_LIMIT),
        name="ffn_ln",
    )(x, wg, wu, wd, g, b)


def _proj_kernel(x_ref, w_ref, *rest, act, n_inert, tm):
    o_ref = rest[-1]
    x = x_ref[...]
    for c in range(MIX_W // MXU_COLS):
        cols = slice(c * MXU_COLS, (c + 1) * MXU_COLS)
        acc = _dot(x, w_ref[:, cols])
        if act == "silu":
            y = acc * jax.nn.sigmoid(acc)
        elif act == "gate":
            y = jax.nn.sigmoid(acc + rest[0][:, cols])
        elif act == "log_forget":
            l0 = rest[0][0:1, cols]
            l1 = rest[0][1:2, cols]
            m = jnp.maximum(l0, l1)
            e0 = jnp.exp(l0 - m)
            e1 = jnp.exp(l1 - m)
            lb = e0 / (e0 + e1)
            y = jnp.log(lb + (1.0 - lb) * jax.nn.sigmoid(acc))
            if n_inert:
                row = pl.program_id(0) * tm + lax.broadcasted_iota(jnp.int32, y.shape, 0)
                y = jnp.where(row >= n_inert, y, 0.0)
        else:
            y = acc
        o_ref[:, cols] = y.astype(o_ref.dtype)


def _proj(xb, w_in, extra, *, act, col_block, n_blocks, out_dtype, tm, n_inert=0):
    n, d = xb.shape
    assert n % tm == 0
    in_specs = [pl.BlockSpec((tm, d), lambda i, j: (i, 0)),
                pl.BlockSpec((d, MIX_W), lambda i, j: (0, col_block(j)))]
    args = [xb, w_in]
    if act == "gate":
        in_specs.append(pl.BlockSpec((1, MIX_W), lambda i, j: (0, j)))
        args.append(extra)
    elif act == "log_forget":
        in_specs.append(pl.BlockSpec(extra.shape, lambda i, j: (0, 0)))
        args.append(extra)
    return pl.pallas_call(
        functools.partial(_proj_kernel, act=act, n_inert=n_inert, tm=tm),
        grid=(n // tm, n_blocks),
        in_specs=in_specs,
        out_specs=pl.BlockSpec((tm, MIX_W), lambda i, j: (i, j)),
        out_shape=jax.ShapeDtypeStruct((n, n_blocks * MIX_W), out_dtype),
        compiler_params=pltpu.CompilerParams(
            dimension_semantics=("parallel", "arbitrary"), vmem_limit_bytes=VMEM_LIMIT),
        name="proj_" + act,
    )(*args)


def _proj_all(xb, w_in, b_gate, lb_logits, *, tm, n_inert):
    d = xb.shape[1]
    silu = _proj(xb, w_in, None, act="silu", col_block=lambda j: 3 * j, n_blocks=2, out_dtype=BF16, tm=tm)
    log_f = _proj(xb, w_in, lb_logits, act="log_forget", col_block=lambda j: 1, n_blocks=1, out_dtype=F32,
                  tm=tm, n_inert=n_inert)
    plain = _proj(xb, w_in, None, act="none", col_block=lambda j: jnp.where(j > 0, j + 3, 2), n_blocks=4,
                  out_dtype=BF16, tm=tm)
    gates = _proj(xb, w_in, b_gate, act="gate", col_block=lambda j: j + 7, n_blocks=2 * d // MIX_W,
                  out_dtype=BF16, tm=tm)
    return silu, log_f, plain, gates


def _hgrn_levels(tt):
    return [m for m in (16, 32, 64, 128, 256, 512) if m <= tt]


def _hgrn_kernel(q_ref, lf_ref, v_ref, og_ref, gain_ref, st0_ref, tri_ref, lvl_ref, o_ref, stf_ref, st_ref,
                 fpad_ref, kpad_ref, vpad_ref, *, tt, lt, hpb):
    t_id = pl.program_id(2)

    @pl.when(t_id == 0)
    def _():
        st_ref[...] = st0_ref[...]

    row = lax.broadcasted_iota(jnp.int32, (tt, HEAD_DIM), 0)
    sub_start = (row & (SUB - 1)) == 0
    zeros = jnp.zeros((SUB, HEAD_DIM), F32)
    lvl = lvl_ref[...]

    for hd in range(hpb):
        lanes = slice(hd * HEAD_DIM, (hd + 1) * HEAD_DIM)
        qb = q_ref[:, lanes]
        q = qb.astype(F32)
        g2 = lf_ref[:, lanes] * LOG2_E
        vb = v_ref[:, lanes]
        v = vb.astype(F32)
        f = jnp.exp2(g2)
        kk = 1.0 - f
        kb = kk.astype(BF16)
        g_hi, g_lo = _split_bf16(g2)

        fpad_ref[hd, 0:SUB, :] = zeros
        kpad_ref[hd, 0:SUB, :] = zeros
        vpad_ref[hd, 0:SUB, :] = zeros
        fpad_ref[hd, SUB:, :] = jnp.where(sub_start, 0.0, f)
        kpad_ref[hd, SUB:, :] = kk
        vpad_ref[hd, SUB:, :] = v
        o = jnp.sum(q * kk, axis=-1, keepdims=True) * v
        u = q
        for d in range(1, SUB):
            u = u * fpad_ref[hd, pl.ds(SUB - d + 1, tt), :]
            k_d = kpad_ref[hd, pl.ds(SUB - d, tt), :]
            v_d = vpad_ref[hd, pl.ds(SUB - d, tt), :]
            o = o + jnp.sum(u * k_d, axis=-1, keepdims=True) * v_d

        for s in range(tt // lt):
            rows = slice(s * lt, (s + 1) * lt)
            b = _dot(tri_ref[...], jnp.concatenate([g_hi[rows], g_lo[rows]], axis=0))

            scores = jnp.zeros((lt, lt), F32)
            for idx, m in enumerate(_hgrn_levels(lt)):
                mid = b.reshape(lt // m, m, HEAD_DIM)[:, m // 2 - 1:m // 2, :]
                b_mid = jnp.broadcast_to(mid, (lt // m, m, HEAD_DIM)).reshape(lt, HEAD_DIM)
                e = jnp.exp2(-jnp.abs(b - b_mid)).astype(BF16)
                scores = jnp.where(lvl == idx, _dot_nt(qb[rows] * e, kb[rows] * e), scores)
            st = st_ref[hd]
            o_s = (o[rows] + _dot(scores.astype(BF16), vb[rows])
                   + _dot_nt(qb[rows] * jnp.exp2(b).astype(BF16), st.astype(BF16)))

            b_end = b[lt - 1:lt, :]
            k_end = kb[rows] * jnp.exp2(b_end - b).astype(BF16)
            st_ref[hd] = st * jnp.exp2(b_end) + _dot_tn(vb[rows], k_end)

            o_s = o_s * lax.rsqrt(jnp.mean(o_s * o_s, axis=-1, keepdims=True) + RMS_EPS)
            o_ref[rows, lanes] = (o_s * gain_ref[:, lanes] * og_ref[rows, lanes].astype(F32)).astype(BF16)

    @pl.when(t_id == pl.num_programs(2) - 1)
    def _():
        stf_ref[...] = st_ref[...]


def _hgrn(q_og, lf, plain, gain, st0, *, tt, hpb):
    bsz, length, _ = lf.shape
    lt = min(tt, HG_LEVEL_TILE)
    levels = _hgrn_levels(lt)
    assert length % tt == 0 and tt % lt == 0 and levels[-1] == lt and SUB * 2 == levels[0] and HEADS % hpb == 0
    r = lax.broadcasted_iota(jnp.int32, (lt, 2 * lt), 0)
    c = lax.broadcasted_iota(jnp.int32, (lt, 2 * lt), 1) % lt
    tri = (c <= r).astype(BF16)
    r = lax.broadcasted_iota(jnp.int32, (lt, lt), 0)
    c = lax.broadcasted_iota(jnp.int32, (lt, lt), 1)
    lvl = jnp.full((lt, lt), -1, jnp.int32)
    for idx, m in reversed(list(enumerate(levels))):
        lvl = jnp.where(r // m == c // m, idx, lvl)
    lvl = jnp.where(jnp.logical_or(r // SUB == c // SUB, c > r), -1, lvl)
    groups = HEADS // hpb
    width = hpb * HEAD_DIM
    tok = lambda off: pl.BlockSpec((None, tt, width), lambda b, h, t: (b, t, h + off))
    const = lambda a: pl.BlockSpec(a.shape, lambda b, h, t: (0, 0))
    return pl.pallas_call(
        functools.partial(_hgrn_kernel, tt=tt, lt=lt, hpb=hpb),
        grid=(bsz, groups, length // tt),
        in_specs=[tok(0), tok(0), tok(0), tok(groups),
                  pl.BlockSpec((1, width), lambda b, h, t: (0, h)),
                  pl.BlockSpec((hpb, HEAD_DIM, HEAD_DIM), lambda b, h, t: (h, 0, 0)),
                  const(tri), const(lvl)],
        out_specs=[tok(0), pl.BlockSpec((None, hpb, HEAD_DIM, HEAD_DIM), lambda b, h, t: (b, h, 0, 0))],
        out_shape=[jax.ShapeDtypeStruct((bsz, length, MIX_W), BF16),
                   jax.ShapeDtypeStruct((bsz, HEADS, HEAD_DIM, HEAD_DIM), F32)],
        scratch_shapes=[pltpu.VMEM((hpb, HEAD_DIM, HEAD_DIM), F32)]
        + [pltpu.VMEM((hpb, tt + SUB, HEAD_DIM), F32)] * 3,
        compiler_params=pltpu.CompilerParams(
            dimension_semantics=("parallel", "parallel", "arbitrary"), vmem_limit_bytes=VMEM_LIMIT),
        name="hgrn2",
    )(q_og, lf, plain, q_og, gain, st0, tri, lvl)


def _sb_kernel(q_ref, k_ref, v_ref, kp_ref, vp_ref, tri_ref, o_ref, acc_ref, rem_ref, *, ts, n_sub):
    qi = pl.program_id(2)
    r_i = lax.broadcasted_iota(jnp.int32, (ts, ts), 0)
    c_i = lax.broadcasted_iota(jnp.int32, (ts, ts), 1)
    causal = c_i < r_i
    prefix_valid = jnp.logical_and(c_i >= PAD, c_i < BLOCK)

    def tile(j):
        start = pl.multiple_of(j * ts, ts)
        return k_ref[pl.ds(start, ts), :], v_ref[pl.ds(start, ts), :]

    def log_weights(q, k_t, mask):
        z = _dot_nt(q, k_t)
        log_beta = jnp.minimum(z, 0.0) - jnp.log(1.0 + jnp.exp(-jnp.abs(z)))
        log_rest = log_beta - z
        if mask is not None:
            log_rest = jnp.where(mask, log_rest, 0.0)
        hi, lo = _split_bf16(log_rest)
        later = _dot(jnp.concatenate([hi, lo], axis=1), tri_ref[...])
        return log_beta + later, jnp.sum(log_rest, axis=-1, keepdims=True)

    def weights(log_w, mask):
        w = jnp.exp(log_w)
        if mask is not None:
            w = jnp.where(mask, w, 0.0)
        return w.astype(BF16)

    def first_visits(sub0_has_prev):
        for s in range(n_sub):
            q = q_ref[s * ts:(s + 1) * ts, :]
            g = qi * n_sub + s
            k_t, v_t = tile(g)
            log_w, rem = log_weights(q, k_t, causal)
            acc = _dot(weights(log_w, causal), v_t)
            if s > 0 or sub0_has_prev:
                k_t, v_t = tile(g - 1)
                log_w, total = log_weights(q, k_t, None)
                acc = acc + _dot(weights(log_w + rem, None), v_t)
                rem = rem + total
            acc_ref[s] = acc
            rem_ref[s] = rem

    @pl.when(qi > 0)
    def _():
        first_visits(True)

    @pl.when(qi == 0)
    def _():
        first_visits(False)

    def finish(s):
        q = q_ref[s * ts:(s + 1) * ts, :]

        def visit(k_t, v_t, mask):
            log_w, total = log_weights(q, k_t, mask)
            rem = rem_ref[s]
            acc_ref[s] += _dot(weights(log_w + rem, mask), v_t)
            rem = rem + total
            rem_ref[s] = rem
            return jnp.max(rem)

        def cond(state):
            j, live = state
            return jnp.logical_and(j >= 0, live > SB_DEAD_LOG)

        def body(state):
            j, _ = state
            return j - 1, visit(*tile(j), None)

        _, live = lax.while_loop(cond, body, (qi * n_sub + s - 2, jnp.max(rem_ref[s])))

        @pl.when(live > SB_DEAD_LOG)
        def _():
            visit(kp_ref[...], vp_ref[...], prefix_valid)

    @pl.when(jnp.max(rem_ref[...]) > SB_DEAD_LOG)
    def _():
        for s in range(n_sub):
            finish(s)

    for s in range(n_sub):
        o_ref[s * ts:(s + 1) * ts, :] = acc_ref[s].astype(BF16)


def _stick_breaking(plain, kp, vp, *, n_sub):
    bsz, length, _ = plain.shape
    ts = SB_TILE
    tq = ts * n_sub
    assert length % tq == 0
    r = lax.broadcasted_iota(jnp.int32, (2 * ts, ts), 0) % ts
    c = lax.broadcasted_iota(jnp.int32, (2 * ts, ts), 1)
    tri = (r > c).astype(BF16)
    seq = lambda off: pl.BlockSpec((None, length, HEAD_DIM), lambda b, h, i: (b, 0, h + off))
    pre = pl.BlockSpec((ts, HEAD_DIM), lambda b, h, i: (0, h))
    return pl.pallas_call(
        functools.partial(_sb_kernel, ts=ts, n_sub=n_sub),
        grid=(bsz, HEADS, length // tq),
        in_specs=[pl.BlockSpec((None, tq, HEAD_DIM), lambda b, h, i: (b, i, h + HEADS)),
                  seq(2 * HEADS), seq(3 * HEADS), pre, pre,
                  pl.BlockSpec((2 * ts, ts), lambda b, h, i: (0, 0))],
        out_specs=pl.BlockSpec((None, tq, HEAD_DIM), lambda b, h, i: (b, i, h)),
        out_shape=jax.ShapeDtypeStruct((bsz, length, MIX_W), BF16),
        scratch_shapes=[pltpu.VMEM((n_sub, ts, HEAD_DIM), F32), pltpu.VMEM((n_sub, ts, 1), F32)],
        compiler_params=pltpu.CompilerParams(
            dimension_semantics=("parallel", "parallel", "arbitrary"), vmem_limit_bytes=VMEM_LIMIT),
        name="stick_breaking",
    )(plain, plain, plain, kp, vp, tri)


def _merge_ln_kernel(h_ref, ohg_ref, osb_ref, ghg_ref, gsb_ref, php_ref, psb_ref, wo_ref, g_ref, b_ref, o_ref, *,
                     n_sub):
    tr = h_ref.shape[0] // n_sub
    for s in range(n_sub):
        rows = slice(s * tr, (s + 1) * tr)
        y = (ghg_ref[rows, :].astype(F32) * _dot(ohg_ref[rows, :], php_ref[...])
             + gsb_ref[rows, :].astype(F32) * _dot(osb_ref[rows, :], psb_ref[...]))
        mix = _dot(y.astype(BF16), wo_ref[...])
        o_ref[rows, :] = _layer_norm(DN_ALPHA * h_ref[rows, :] + mix, g_ref[...], b_ref[...])


def _merge_ln(h, o_hg, o_sb, gates, p_hg, p_sb, w_out, g, b, *, tm):
    n, d = h.shape
    assert n % tm == 0
    row = lambda w: pl.BlockSpec((tm, w), lambda i: (i, 0))
    full = lambda a: pl.BlockSpec(a.shape, lambda i: (0, 0), pipeline_mode=pl.Buffered(1))
    return pl.pallas_call(
        functools.partial(_merge_ln_kernel, n_sub=2 if tm % 512 == 0 else 1),
        grid=(n // tm,),
        in_specs=[row(d), row(MIX_W), row(MIX_W),
                  pl.BlockSpec((tm, d), lambda i: (i, 0)), pl.BlockSpec((tm, d), lambda i: (i, 1)),
                  full(p_hg), full(p_sb), full(w_out), full(g), full(b)],
        out_specs=row(d),
        out_shape=jax.ShapeDtypeStruct((n, d), F32),
        compiler_params=pltpu.CompilerParams(
            dimension_semantics=("parallel",), vmem_limit_bytes=VMEM_LIMIT),
        name="merge_ln",
    )(h, o_hg, o_sb, gates, gates, p_hg, p_sb, w_out, g, b)


def _pick(n, pref):
    t = min(n, pref)
    while n % t:
        t //= 2
    return t


def kernel(x, meta, ln1_g, ln1_b, ffn1_w_gate, ffn1_w_up, ffn1_w_down, w_in, b_gate, hg_lb_logits, hg_norm_g, w_proj_hg, w_proj_sb, w_out, ln2_g, ln2_b, ffn2_w_gate, ffn2_w_up, ffn2_w_down, ln3_g, ln3_b):
    bsz, seq, d = x.shape
    assert ln1_g.shape[0] == DEPTH and hg_lb_logits.shape == (DEPTH + 1, MIX_W) and (2 * d) % MIX_W == 0
    n = bsz * seq
    tm = _pick(n, 1024)
    tf = _pick(ffn1_w_gate.shape[2], 512)
    bf = lambda a: a[0].astype(BF16)
    row = lambda a: a[0].reshape(1, -1)

    w1 = (bf(ffn1_w_gate), bf(ffn1_w_up), bf(ffn1_w_down), row(ln1_g), row(ln1_b))
    w2 = (bf(ffn2_w_gate), bf(ffn2_w_up), bf(ffn2_w_down), row(ln3_g), row(ln3_b))
    col = lax.broadcasted_iota(jnp.int32, (1, w_in.shape[2]), 1)
    is_sq = jnp.logical_and(col >= 4 * MIX_W, col < 5 * MIX_W)
    w_in_b = (w_in[0] * jnp.where(is_sq, 1.0 / math.sqrt(HEAD_DIM), 1.0)).astype(BF16)
    bg = row(b_gate)
    gain = row(hg_norm_g)

    hp = jnp.concatenate([jnp.zeros((PAD, d), x.dtype), meta.astype(x.dtype)], axis=0)
    _, hpb = _ffn_ln(hp, *w1, tm=BLOCK, tf=tf, emit_bf16=True)
    _, lf_p, plain_p, _ = _proj_all(hpb, w_in_b, bg, hg_lb_logits, tm=BLOCK, n_inert=PAD)
    _, st0 = _hgrn(jnp.zeros((1, BLOCK, 2 * MIX_W), BF16), lf_p[None], plain_p[None], gain,
                   jnp.zeros((HEADS, HEAD_DIM, HEAD_DIM), F32), tt=BLOCK, hpb=HG_HEADS_PER_STEP)
    pad_rows = ((0, SB_TILE - BLOCK), (0, 0))
    kp = jnp.pad(plain_p[:, 2 * MIX_W:3 * MIX_W], pad_rows)
    vp = jnp.pad(plain_p[:, 3 * MIX_W:], pad_rows)

    h, hb = _ffn_ln(x.reshape(n, d), *w1, tm=tm, tf=tf, emit_bf16=True)
    q_og, lf, plain, gates = _proj_all(hb, w_in_b, bg, hg_lb_logits, tm=_pick(n, 2048), n_inert=0)
    seq3 = lambda a: a.reshape(bsz, seq, a.shape[-1])
    o_hg, _ = _hgrn(seq3(q_og), seq3(lf), seq3(plain), gain, st0[0], tt=_pick(seq, 256),
                    hpb=HG_HEADS_PER_STEP)
    o_sb = _stick_breaking(seq3(plain), kp, vp, n_sub=_pick(seq // SB_TILE, 4))
    h = _merge_ln(h, o_hg.reshape(n, MIX_W), o_sb.reshape(n, MIX_W), gates, bf(w_proj_hg), bf(w_proj_sb),
                  bf(w_out), row(ln2_g), row(ln2_b), tm=_pick(n, 512))
    (h,) = _ffn_ln(h, *w2, tm=tm, tf=tf, emit_bf16=False)
    return h.reshape(bsz, seq, d)
```

```python
import functools
import math

import jax
import jax.numpy as jnp
from jax import lax
from jax.experimental import pallas as pl
from jax.experimental.pallas import tpu as pltpu

N_META = 16
BLOCK = 128
PAD = BLOCK - N_META
HEADS = 8
HEAD_DIM = 128
MIX_W = HEADS * HEAD_DIM
SUB = 8
HG_HEADS_PER_STEP = 8
HG_LEVEL_TILE = 128
LOG2_E = math.log2(math.e)
SB_TILE = 256
LN_EPS = 1e-5
RMS_EPS = 1e-6
DEPTH = 1
DN_ALPHA = (2.0 * DEPTH) ** 0.25
SB_DEAD_LOG = -104.0
MXU_COLS = 256
VMEM_LIMIT = 60 * 1024 * 1024

F32 = jnp.float32
BF16 = jnp.bfloat16


def _layer_norm(y, g, b):
    mu = jnp.mean(y, axis=-1, keepdims=True)
    yc = y - mu
    var = jnp.mean(yc * yc, axis=-1, keepdims=True)
    return yc * lax.rsqrt(var + LN_EPS) * g + b


def _dot(a, b):
    return jnp.dot(a, b, preferred_element_type=F32)


def _dot_nt(a, b):
    return lax.dot_general(a, b, (((1,), (1,)), ((), ())), preferred_element_type=F32)


def _dot_tn(a, b):
    return lax.dot_general(a, b, (((0,), (0,)), ((), ())), preferred_element_type=F32)


def _split_bf16(x):
    hi = x.astype(BF16)
    return hi, (x - hi.astype(F32)).astype(BF16)


def _ffn_ln_kernel(x_hbm, wg_ref, wu_ref, wd_ref, g_ref, b_ref, o_ref, *rest, emit_bf16, tm):
    xb_ref, xbuf_ref, sem = rest
    ob_ref = xb_ref if emit_bf16 else None
    i = pl.program_id(0)
    j = pl.program_id(1)
    n_i = pl.num_programs(0)
    n_j = pl.num_programs(1)

    def x_copy(tile):
        return pltpu.make_async_copy(x_hbm.at[pl.ds(pl.multiple_of(tile * tm, tm), tm), :], xbuf_ref, sem)

    @pl.when(jnp.logical_and(i == 0, j == 0))
    def _():
        x_copy(0).start()

    @pl.when(j == 0)
    def _():
        x_copy(i).wait()
        x = xbuf_ref[...]
        xb_ref[...] = x.astype(BF16)
        o_ref[...] = (DN_ALPHA / 0.5) * x

    @pl.when(jnp.logical_and(j == jnp.minimum(1, n_j - 1), i + 1 < n_i))
    def _():
        x_copy(i + 1).start()

    xb = xb_ref[...]
    gate = _dot(xb, wg_ref[...])
    up = _dot(xb, wu_ref[...])
    act = (gate * jax.nn.sigmoid(gate) * up).astype(BF16)
    o_ref[...] += _dot(act, wd_ref[...])

    @pl.when(j == n_j - 1)
    def _():
        y = _layer_norm(0.5 * o_ref[...], g_ref[...], b_ref[...])
        o_ref[...] = y
        if ob_ref is not None:
            ob_ref[...] = y.astype(BF16)


def _ffn_ln(x, wg, wu, wd, g, b, *, tm, tf, emit_bf16):
    n, d = x.shape
    dff = wg.shape[1]
    assert n % tm == 0 and dff % tf == 0
    rows = pl.BlockSpec((tm, d), lambda i, j: (i, 0))
    out_shape = [jax.ShapeDtypeStruct((n, d), F32)]
    if emit_bf16:
        out_shape.append(jax.ShapeDtypeStruct((n, d), BF16))
    return pl.pallas_call(
        functools.partial(_ffn_ln_kernel, emit_bf16=emit_bf16, tm=tm),
        grid=(n // tm, dff // tf),
        in_specs=[
            pl.BlockSpec(memory_space=pl.ANY),
            pl.BlockSpec((d, tf), lambda i, j: (0, j)),
            pl.BlockSpec((d, tf), lambda i, j: (0, j)),
            pl.BlockSpec((tf, d), lambda i, j: (j, 0)),
            pl.BlockSpec((1, d), lambda i, j: (0, 0)),
            pl.BlockSpec((1, d), lambda i, j: (0, 0)),
        ],
        out_specs=[rows] * len(out_shape),
        out_shape=out_shape,
        scratch_shapes=([] if emit_bf16 else [pltpu.VMEM((tm, d), BF16)])
        + [pltpu.VMEM((tm, d), F32), pltpu.SemaphoreType.DMA(())],
        compiler_params=pltpu.CompilerParams(
            dimension_semantics=("arbitrary", "arbitrary"), vmem_limit_bytes=VMEM_LIMIT),
        name="ffn_ln",
    )(x, wg, wu, wd, g, b)


def _proj_kernel(x_ref, w_ref, *rest, act, n_inert, tm):
    o_ref = rest[-1]
    x = x_ref[...]
    for c in range(MIX_W // MXU_COLS):
        cols = slice(c * MXU_COLS, (c + 1) * MXU_COLS)
        acc = _dot(x, w_ref[:, cols])
        if act == "silu":
            y = acc * jax.nn.sigmoid(acc)
        elif act == "gate":
            y = jax.nn.sigmoid(acc + rest[0][:, cols])
        elif act == "log_forget":
            l0 = rest[0][0:1, cols]
            l1 = rest[0][1:2, cols]
            m = jnp.maximum(l0, l1)
            e0 = jnp.exp(l0 - m)
            e1 = jnp.exp(l1 - m)
            lb = e0 / (e0 + e1)
            y = jnp.log(lb + (1.0 - lb) * jax.nn.sigmoid(acc))
            if n_inert:
                row = pl.program_id(0) * tm + lax.broadcasted_iota(jnp.int32, y.shape, 0)
                y = jnp.where(row >= n_inert, y, 0.0)
        else:
            y = acc
        o_ref[:, cols] = y.astype(o_ref.dtype)


def _proj(xb, w_in, extra, *, act, col_block, n_blocks, out_dtype, tm, n_inert=0):
    n, d = xb.shape
    assert n % tm == 0
    in_specs = [pl.BlockSpec((tm, d), lambda i, j: (i, 0)),
                pl.BlockSpec((d, MIX_W), lambda i, j: (0, col_block(j)))]
    args = [xb, w_in]
    if act == "gate":
        in_specs.append(pl.BlockSpec((1, MIX_W), lambda i, j: (0, j)))
        args.append(extra)
    elif act == "log_forget":
        in_specs.append(pl.BlockSpec(extra.shape, lambda i, j: (0, 0)))
        args.append(extra)
    return pl.pallas_call(
        functools.partial(_proj_kernel, act=act, n_inert=n_inert, tm=tm),
        grid=(n // tm, n_blocks),
        in_specs=in_specs,
        out_specs=pl.BlockSpec((tm, MIX_W), lambda i, j: (i, j)),
        out_shape=jax.ShapeDtypeStruct((n, n_blocks * MIX_W), out_dtype),
        compiler_params=pltpu.CompilerParams(
            dimension_semantics=("parallel", "arbitrary"), vmem_limit_bytes=VMEM_LIMIT),
        name="proj_" + act,
    )(*args)


def _proj_all(xb, w_in, b_gate, lb_logits, *, tm, n_inert):
    d = xb.shape[1]
    silu = _proj(xb, w_in, None, act="silu", col_block=lambda j: 3 * j, n_blocks=2, out_dtype=BF16, tm=tm)
    log_f = _proj(xb, w_in, lb_logits, act="log_forget", col_block=lambda j: 1, n_blocks=1, out_dtype=F32,
                  tm=tm, n_inert=n_inert)
    plain = _proj(xb, w_in, None, act="none", col_block=lambda j: jnp.where(j > 0, j + 3, 2), n_blocks=4,
                  out_dtype=BF16, tm=tm)
    gates = _proj(xb, w_in, b_gate, act="gate", col_block=lambda j: j + 7, n_blocks=2 * d // MIX_W,
                  out_dtype=BF16, tm=tm)
    return silu, log_f, plain, gates


def _hgrn_levels(tt):
    return [m for m in (16, 32, 64, 128, 256, 512) if m <= tt]


def _hgrn_kernel(q_ref, lf_ref, v_ref, og_ref, gain_ref, st0_ref, tri_ref, lvl_ref, o_ref, stf_ref, st_ref,
                 fpad_ref, kpad_ref, vpad_ref, *, tt, lt, hpb):
    t_id = pl.program_id(2)

    @pl.when(t_id == 0)
    def _():
        st_ref[...] = st0_ref[...]

    row = lax.broadcasted_iota(jnp.int32, (tt, HEAD_DIM), 0)
    sub_start = (row & (SUB - 1)) == 0
    zeros = jnp.zeros((SUB, HEAD_DIM), F32)
    lvl = lvl_ref[...]

    for hd in range(hpb):
        lanes = slice(hd * HEAD_DIM, (hd + 1) * HEAD_DIM)
        qb = q_ref[:, lanes]
        q = qb.astype(F32)
        g2 = lf_ref[:, lanes] * LOG2_E
        vb = v_ref[:, lanes]
        v = vb.astype(F32)
        f = jnp.exp2(g2)
        kk = 1.0 - f
        kb = kk.astype(BF16)
        g_hi, g_lo = _split_bf16(g2)

        fpad_ref[hd, 0:SUB, :] = zeros
        kpad_ref[hd, 0:SUB, :] = zeros
        vpad_ref[hd, 0:SUB, :] = zeros
        fpad_ref[hd, SUB:, :] = jnp.where(sub_start, 0.0, f)
        kpad_ref[hd, SUB:, :] = kk
        vpad_ref[hd, SUB:, :] = v
        o = jnp.sum(q * kk, axis=-1, keepdims=True) * v
        u = q
        for d in range(1, SUB):
            u = u * fpad_ref[hd, pl.ds(SUB - d + 1, tt), :]
            k_d = kpad_ref[hd, pl.ds(SUB - d, tt), :]
            v_d = vpad_ref[hd, pl.ds(SUB - d, tt), :]
            o = o + jnp.sum(u * k_d, axis=-1, keepdims=True) * v_d

        for s in range(tt // lt):
            rows = slice(s * lt, (s + 1) * lt)
            b = _dot(tri_ref[...], jnp.concatenate([g_hi[rows], g_lo[rows]], axis=0))

            scores = jnp.zeros((lt, lt), F32)
            for idx, m in enumerate(_hgrn_levels(lt)):
                mid = b.reshape(lt // m, m, HEAD_DIM)[:, m // 2 - 1:m // 2, :]
                b_mid = jnp.broadcast_to(mid, (lt // m, m, HEAD_DIM)).reshape(lt, HEAD_DIM)
                e = jnp.exp2(-jnp.abs(b - b_mid)).astype(BF16)
                scores = jnp.where(lvl == idx, _dot_nt(qb[rows] * e, kb[rows] * e), scores)
            st = st_ref[hd]
            o_s = (o[rows] + _dot(scores.astype(BF16), vb[rows])
                   + _dot_nt(qb[rows] * jnp.exp2(b).astype(BF16), st.astype(BF16)))

            b_end = b[lt - 1:lt, :]
            k_end = kb[rows] * jnp.exp2(b_end - b).astype(BF16)
            st_ref[hd] = st * jnp.exp2(b_end) + _dot_tn(vb[rows], k_end)

            o_s = o_s * lax.rsqrt(jnp.mean(o_s * o_s, axis=-1, keepdims=True) + RMS_EPS)
            o_ref[rows, lanes] = (o_s * gain_ref[:, lanes] * og_ref[rows, lanes].astype(F32)).astype(BF16)

    @pl.when(t_id == pl.num_programs(2) - 1)
    def _():
        stf_ref[...] = st_ref[...]


def _hgrn(q_og, lf, plain, gain, st0, *, tt, hpb):
    bsz, length, _ = lf.shape
    lt = min(tt, HG_LEVEL_TILE)
    levels = _hgrn_levels(lt)
    assert length % tt == 0 and tt % lt == 0 and levels[-1] == lt and SUB * 2 == levels[0] and HEADS % hpb == 0
    r = lax.broadcasted_iota(jnp.int32, (lt, 2 * lt), 0)
    c = lax.broadcasted_iota(jnp.int32, (lt, 2 * lt), 1) % lt
    tri = (c <= r).astype(BF16)
    r = lax.broadcasted_iota(jnp.int32, (lt, lt), 0)
    c = lax.broadcasted_iota(jnp.int32, (lt, lt), 1)
    lvl = jnp.full((lt, lt), -1, jnp.int32)
    for idx, m in reversed(list(enumerate(levels))):
        lvl = jnp.where(r // m == c // m, idx, lvl)
    lvl = jnp.where(jnp.logical_or(r // SUB == c // SUB, c > r), -1, lvl)
    groups = HEADS // hpb
    width = hpb * HEAD_DIM
    tok = lambda off: pl.BlockSpec((None, tt, width), lambda b, h, t: (b, t, h + off))
    const = lambda a: pl.BlockSpec(a.shape, lambda b, h, t: (0, 0))
    return pl.pallas_call(
        functools.partial(_hgrn_kernel, tt=tt, lt=lt, hpb=hpb),
        grid=(bsz, groups, length // tt),
        in_specs=[tok(0), tok(0), tok(0), tok(groups),
                  pl.BlockSpec((1, width), lambda b, h, t: (0, h)),
                  pl.BlockSpec((hpb, HEAD_DIM, HEAD_DIM), lambda b, h, t: (h, 0, 0)),
                  const(tri), const(lvl)],
        out_specs=[tok(0), pl.BlockSpec((None, hpb, HEAD_DIM, HEAD_DIM), lambda b, h, t: (b, h, 0, 0))],
        out_shape=[jax.ShapeDtypeStruct((bsz, length, MIX_W), BF16),
                   jax.ShapeDtypeStruct((bsz, HEADS, HEAD_DIM, HEAD_DIM), F32)],
        scratch_shapes=[pltpu.VMEM((hpb, HEAD_DIM, HEAD_DIM), F32)]
        + [pltpu.VMEM((hpb, tt + SUB, HEAD_DIM), F32)] * 3,
        compiler_params=pltpu.CompilerParams(
            dimension_semantics=("parallel", "parallel", "arbitrary"), vmem_limit_bytes=VMEM_LIMIT),
        name="hgrn2",
    )(q_og, lf, plain, q_og, gain, st0, tri, lvl)


def _sb_kernel(q_ref, k_ref, v_ref, kp_ref, vp_ref, tri_ref, o_ref, acc_ref, rem_ref, *, ts, n_sub):
    qi = pl.program_id(2)
    r_i = lax.broadcasted_iota(jnp.int32, (ts, ts), 0)
    c_i = lax.broadcasted_iota(jnp.int32, (ts, ts), 1)
    causal = c_i < r_i
    prefix_valid = jnp.logical_and(c_i >= PAD, c_i < BLOCK)

    def tile(j):
        start = pl.multiple_of(j * ts, ts)
        return k_ref[pl.ds(start, ts), :], v_ref[pl.ds(start, ts), :]

    def log_weights(q, k_t, mask):
        z = _dot_nt(q, k_t)
        log_beta = jnp.minimum(z, 0.0) - jnp.log(1.0 + jnp.exp(-jnp.abs(z)))
        log_rest = log_beta - z
        if mask is not None:
            log_rest = jnp.where(mask, log_rest, 0.0)
        hi, lo = _split_bf16(log_rest)
        later = _dot(jnp.concatenate([hi, lo], axis=1), tri_ref[...])
        return log_beta + later, jnp.sum(log_rest, axis=-1, keepdims=True)

    def weights(log_w, mask):
        w = jnp.exp(log_w)
        if mask is not None:
            w = jnp.where(mask, w, 0.0)
        return w.astype(BF16)

    def first_visits(sub0_has_prev):
        for s in range(n_sub):
            q = q_ref[s * ts:(s + 1) * ts, :]
            g = qi * n_sub + s
            k_t, v_t = tile(g)
            log_w, rem = log_weights(q, k_t, causal)
            acc = _dot(weights(log_w, causal), v_t)
            if s > 0 or sub0_has_prev:
                k_t, v_t = tile(g - 1)
                log_w, total = log_weights(q, k_t, None)
                acc = acc + _dot(weights(log_w + rem, None), v_t)
                rem = rem + total
            acc_ref[s] = acc
            rem_ref[s] = rem

    @pl.when(qi > 0)
    def _():
        first_visits(True)

    @pl.when(qi == 0)
    def _():
        first_visits(False)

    def finish(s):
        q = q_ref[s * ts:(s + 1) * ts, :]

        def visit(k_t, v_t, mask):
            log_w, total = log_weights(q, k_t, mask)
            rem = rem_ref[s]
            acc_ref[s] += _dot(weights(log_w + rem, mask), v_t)
            rem = rem + total
            rem_ref[s] = rem
            return jnp.max(rem)

        def cond(state):
            j, live = state
            return jnp.logical_and(j >= 0, live > SB_DEAD_LOG)

        def body(state):
            j, _ = state
            return j - 1, visit(*tile(j), None)

        _, live = lax.while_loop(cond, body, (qi * n_sub + s - 2, jnp.max(rem_ref[s])))

        @pl.when(live > SB_DEAD_LOG)
        def _():
            visit(kp_ref[...], vp_ref[...], prefix_valid)

    @pl.when(jnp.max(rem_ref[...]) > SB_DEAD_LOG)
    def _():
        for s in range(n_sub):
            finish(s)

    for s in range(n_sub):
        o_ref[s * ts:(s + 1) * ts, :] = acc_ref[s].astype(BF16)


def _stick_breaking(plain, kp, vp, *, n_sub):
    bsz, length, _ = plain.shape
    ts = SB_TILE
    tq = ts * n_sub
    assert length % tq == 0
    r = lax.broadcasted_iota(jnp.int32, (2 * ts, ts), 0) % ts
    c = lax.broadcasted_iota(jnp.int32, (2 * ts, ts), 1)
    tri = (r > c).astype(BF16)
    seq = lambda off: pl.BlockSpec((None, length, HEAD_DIM), lambda b, h, i: (b, 0, h + off))
    pre = pl.BlockSpec((ts, HEAD_DIM), lambda b, h, i: (0, h))
    return pl.pallas_call(
        functools.partial(_sb_kernel, ts=ts, n_sub=n_sub),
        grid=(bsz, HEADS, length // tq),
        in_specs=[pl.BlockSpec((None, tq, HEAD_DIM), lambda b, h, i: (b, i, h + HEADS)),
                  seq(2 * HEADS), seq(3 * HEADS), pre, pre,
                  pl.BlockSpec((2 * ts, ts), lambda b, h, i: (0, 0))],
        out_specs=pl.BlockSpec((None, tq, HEAD_DIM), lambda b, h, i: (b, i, h)),
        out_shape=jax.ShapeDtypeStruct((bsz, length, MIX_W), BF16),
        scratch_shapes=[pltpu.VMEM((n_sub, ts, HEAD_DIM), F32), pltpu.VMEM((n_sub, ts, 1), F32)],
        compiler_params=pltpu.CompilerParams(
            dimension_semantics=("parallel", "parallel", "arbitrary"), vmem_limit_bytes=VMEM_LIMIT),
        name="stick_breaking",
    )(plain, plain, plain, kp, vp, tri)


def _merge_ln_kernel(h_ref, ohg_ref, osb_ref, ghg_ref, gsb_ref, php_ref, psb_ref, wo_ref, g_ref, b_ref, o_ref, *,
                     n_sub):
    tr = h_ref.shape[0] // n_sub
    for s in range(n_sub):
        rows = slice(s * tr, (s + 1) * tr)
        y = (ghg_ref[rows, :].astype(F32) * _dot(ohg_ref[rows, :], php_ref[...])
             + gsb_ref[rows, :].astype(F32) * _dot(osb_ref[rows, :], psb_ref[...]))
        mix = _dot(y.astype(BF16), wo_ref[...])
        o_ref[rows, :] = _layer_norm(DN_ALPHA * h_ref[rows, :] + mix, g_ref[...], b_ref[...])


def _merge_ln(h, o_hg, o_sb, gates, p_hg, p_sb, w_out, g, b, *, tm):
    n, d = h.shape
    assert n % tm == 0
    row = lambda w: pl.BlockSpec((tm, w), lambda i: (i, 0))
    full = lambda a: pl.BlockSpec(a.shape, lambda i: (0, 0), pipeline_mode=pl.Buffered(1))
    return pl.pallas_call(
        functools.partial(_merge_ln_kernel, n_sub=2 if tm % 512 == 0 else 1),
        grid=(n // tm,),
        in_specs=[row(d), row(MIX_W), row(MIX_W),
                  pl.BlockSpec((tm, d), lambda i: (i, 0)), pl.BlockSpec((tm, d), lambda i: (i, 1)),
                  full(p_hg), full(p_sb), full(w_out), full(g), full(b)],
        out_specs=row(d),
        out_shape=jax.ShapeDtypeStruct((n, d), F32),
        compiler_params=pltpu.CompilerParams(
            dimension_semantics=("parallel",), vmem_limit_bytes=VMEM_LIMIT),
        name="merge_ln",
    )(h, o_hg, o_sb, gates, gates, p_hg, p_sb, w_out, g, b)


def _pick(n, pref):
    t = min(n, pref)
    while n % t:
        t //= 2
    return t


def kernel(x, meta, ln1_g, ln1_b, ffn1_w_gate, ffn1_w_up, ffn1_w_down, w_in, b_gate, hg_lb_logits, hg_norm_g, w_proj_hg, w_proj_sb, w_out, ln2_g, ln2_b, ffn2_w_gate, ffn2_w_up, ffn2_w_down, ln3_g, ln3_b):
    bsz, seq, d = x.shape
    assert ln1_g.shape[0] == DEPTH and hg_lb_logits.shape == (DEPTH + 1, MIX_W) and (2 * d) % MIX_W == 0
    n = bsz * seq
    tm = _pick(n, 1024)
    tf = _pick(ffn1_w_gate.shape[2], 512)
    bf = lambda a: a[0].astype(BF16)
    row = lambda a: a[0].reshape(1, -1)

    w1 = (bf(ffn1_w_gate), bf(ffn1_w_up), bf(ffn1_w_down), row(ln1_g), row(ln1_b))
    w2 = (bf(ffn2_w_gate), bf(ffn2_w_up), bf(ffn2_w_down), row(ln3_g), row(ln3_b))
    col = lax.broadcasted_iota(jnp.int32, (1, w_in.shape[2]), 1)
    is_sq = jnp.logical_and(col >= 4 * MIX_W, col < 5 * MIX_W)
    w_in_b = (w_in[0] * jnp.where(is_sq, 1.0 / math.sqrt(HEAD_DIM), 1.0)).astype(BF16)
    bg = row(b_gate)
    gain = row(hg_norm_g)

    hp = jnp.concatenate([jnp.zeros((PAD, d), x.dtype), meta.astype(x.dtype)], axis=0)
    _, hpb = _ffn_ln(hp, *w1, tm=BLOCK, tf=tf, emit_bf16=True)
    _, lf_p, plain_p, _ = _proj_all(hpb, w_in_b, bg, hg_lb_logits, tm=BLOCK, n_inert=PAD)
    _, st0 = _hgrn(jnp.zeros((1, BLOCK, 2 * MIX_W), BF16), lf_p[None], plain_p[None], gain,
                   jnp.zeros((HEADS, HEAD_DIM, HEAD_DIM), F32), tt=BLOCK, hpb=HG_HEADS_PER_STEP)
    pad_rows = ((0, SB_TILE - BLOCK), (0, 0))
    kp = jnp.pad(plain_p[:, 2 * MIX_W:3 * MIX_W], pad_rows)
    vp = jnp.pad(plain_p[:, 3 * MIX_W:], pad_rows)

    h, hb = _ffn_ln(x.reshape(n, d), *w1, tm=tm, tf=tf, emit_bf16=True)
    q_og, lf, plain, gates = _proj_all(hb, w_in_b, bg, hg_lb_logits, tm=_pick(n, 2048), n_inert=0)
    seq3 = lambda a: a.reshape(bsz, seq, a.shape[-1])
    o_hg, _ = _hgrn(seq3(q_og), seq3(lf), seq3(plain), gain, st0[0], tt=_pick(seq, 256),
                    hpb=HG_HEADS_PER_STEP)
    o_sb = _stick_breaking(seq3(plain), kp, vp, n_sub=_pick(seq // SB_TILE, 8))
    h = _merge_ln(h, o_hg.reshape(n, MIX_W), o_sb.reshape(n, MIX_W), gates, bf(w_proj_hg), bf(w_proj_sb),
                  bf(w_out), row(ln2_g), row(ln2_b), tm=_pick(n, 512))
    (h,) = _ffn_ln(h, *w2, tm=tm, tf=tf, emit_bf16=False)
    return h.reshape(bsz, seq, d)
```

```python
import functools
import math

import jax
import jax.numpy as jnp
from jax import lax
from jax.experimental import pallas as pl
from jax.experimental.pallas import tpu as pltpu

N_META = 16
BLOCK = 128
PAD = BLOCK - N_META
HEADS = 8
HEAD_DIM = 128
MIX_W = HEADS * HEAD_DIM
SUB = 8
HG_HEADS_PER_STEP = 8
HG_LEVEL_TILE = 128
LOG2_E = math.log2(math.e)
SB_TILE = 256
LN_EPS = 1e-5
RMS_EPS = 1e-6
DEPTH = 1
DN_ALPHA = (2.0 * DEPTH) ** 0.25
SB_DEAD_LOG = -104.0
MXU_COLS = 256
VMEM_LIMIT = 60 * 1024 * 1024

F32 = jnp.float32
BF16 = jnp.bfloat16


def _layer_norm(y, g, b):
    mu = jnp.mean(y, axis=-1, keepdims=True)
    yc = y - mu
    var = jnp.mean(yc * yc, axis=-1, keepdims=True)
    return yc * lax.rsqrt(var + LN_EPS) * g + b


def _dot(a, b):
    return jnp.dot(a, b, preferred_element_type=F32)


def _dot_nt(a, b):
    return lax.dot_general(a, b, (((1,), (1,)), ((), ())), preferred_element_type=F32)


def _dot_tn(a, b):
    return lax.dot_general(a, b, (((0,), (0,)), ((), ())), preferred_element_type=F32)


def _split_bf16(x):
    hi = x.astype(BF16)
    return hi, (x - hi.astype(F32)).astype(BF16)


def _ffn_ln_kernel(x_hbm, wg_ref, wu_ref, wd_ref, g_ref, b_ref, o_ref, *rest, emit_bf16, tm):
    xb_ref, xbuf_ref, sem = rest
    ob_ref = xb_ref if emit_bf16 else None
    i = pl.program_id(0)
    j = pl.program_id(1)
    n_i = pl.num_programs(0)
    n_j = pl.num_programs(1)

    def x_copy(tile):
        return pltpu.make_async_copy(x_hbm.at[pl.ds(pl.multiple_of(tile * tm, tm), tm), :], xbuf_ref, sem)

    @pl.when(jnp.logical_and(i == 0, j == 0))
    def _():
        x_copy(0).start()

    @pl.when(j == 0)
    def _():
        x_copy(i).wait()
        x = xbuf_ref[...]
        xb_ref[...] = x.astype(BF16)
        o_ref[...] = (DN_ALPHA / 0.5) * x

    @pl.when(jnp.logical_and(j == jnp.minimum(1, n_j - 1), i + 1 < n_i))
    def _():
        x_copy(i + 1).start()

    xb = xb_ref[...]
    gate = _dot(xb, wg_ref[...])
    up = _dot(xb, wu_ref[...])
    act = (gate * jax.nn.sigmoid(gate) * up).astype(BF16)
    o_ref[...] += _dot(act, wd_ref[...])

    @pl.when(j == n_j - 1)
    def _():
        y = _layer_norm(0.5 * o_ref[...], g_ref[...], b_ref[...])
        o_ref[...] = y
        if ob_ref is not None:
            ob_ref[...] = y.astype(BF16)


def _ffn_ln(x, wg, wu, wd, g, b, *, tm, tf, emit_bf16):
    n, d = x.shape
    dff = wg.shape[1]
    assert n % tm == 0 and dff % tf == 0
    rows = pl.BlockSpec((tm, d), lambda i, j: (i, 0))
    out_shape = [jax.ShapeDtypeStruct((n, d), F32)]
    if emit_bf16:
        out_shape.append(jax.ShapeDtypeStruct((n, d), BF16))
    return pl.pallas_call(
        functools.partial(_ffn_ln_kernel, emit_bf16=emit_bf16, tm=tm),
        grid=(n // tm, dff // tf),
        in_specs=[
            pl.BlockSpec(memory_space=pl.ANY),
            pl.BlockSpec((d, tf), lambda i, j: (0, j)),
            pl.BlockSpec((d, tf), lambda i, j: (0, j)),
            pl.BlockSpec((tf, d), lambda i, j: (j, 0)),
            pl.BlockSpec((1, d), lambda i, j: (0, 0)),
            pl.BlockSpec((1, d), lambda i, j: (0, 0)),
        ],
        out_specs=[rows] * len(out_shape),
        out_shape=out_shape,
        scratch_shapes=([] if emit_bf16 else [pltpu.VMEM((tm, d), BF16)])
        + [pltpu.VMEM((tm, d), F32), pltpu.SemaphoreType.DMA(())],
        compiler_params=pltpu.CompilerParams(
            dimension_semantics=("arbitrary", "arbitrary"), vmem_limit_bytes=VMEM_LIMIT),
        name="ffn_ln",
    )(x, wg, wu, wd, g, b)


def _proj_kernel(x_ref, w_ref, *rest, act, n_inert, tm):
    o_ref = rest[-1]
    x = x_ref[...]
    for c in range(MIX_W // MXU_COLS):
        cols = slice(c * MXU_COLS, (c + 1) * MXU_COLS)
        acc = _dot(x, w_ref[:, cols])
        if act == "silu":
            y = acc * jax.nn.sigmoid(acc)
        elif act == "gate":
            y = jax.nn.sigmoid(acc + rest[0][:, cols])
        elif act == "log_forget":
            l0 = rest[0][0:1, cols]
            l1 = rest[0][1:2, cols]
            m = jnp.maximum(l0, l1)
            e0 = jnp.exp(l0 - m)
            e1 = jnp.exp(l1 - m)
            lb = e0 / (e0 + e1)
            y = jnp.log(lb + (1.0 - lb) * jax.nn.sigmoid(acc))
            if n_inert:
                row = pl.program_id(0) * tm + lax.broadcasted_iota(jnp.int32, y.shape, 0)
                y = jnp.where(row >= n_inert, y, 0.0)
        else:
            y = acc
        o_ref[:, cols] = y.astype(o_ref.dtype)


def _proj(xb, w_in, extra, *, act, col_block, n_blocks, out_dtype, tm, n_inert=0):
    n, d = xb.shape
    assert n % tm == 0
    in_specs = [pl.BlockSpec((tm, d), lambda i, j: (i, 0)),
                pl.BlockSpec((d, MIX_W), lambda i, j: (0, col_block(j)))]
    args = [xb, w_in]
    if act == "gate":
        in_specs.append(pl.BlockSpec((1, MIX_W), lambda i, j: (0, j)))
        args.append(extra)
    elif act == "log_forget":
        in_specs.append(pl.BlockSpec(extra.shape, lambda i, j: (0, 0)))
        args.append(extra)
    return pl.pallas_call(
        functools.partial(_proj_kernel, act=act, n_inert=n_inert, tm=tm),
        grid=(n // tm, n_blocks),
        in_specs=in_specs,
        out_specs=pl.BlockSpec((tm, MIX_W), lambda i, j: (i, j)),
        out_shape=jax.ShapeDtypeStruct((n, n_blocks * MIX_W), out_dtype),
        compiler_params=pltpu.CompilerParams(
            dimension_semantics=("parallel", "arbitrary"), vmem_limit_bytes=VMEM_LIMIT),
        name="proj_" + act,
    )(*args)


def _proj_all(xb, w_in, b_gate, lb_logits, *, tm, n_inert):
    d = xb.shape[1]
    silu = _proj(xb, w_in, None, act="silu", col_block=lambda j: 3 * j, n_blocks=2, out_dtype=BF16, tm=tm)
    log_f = _proj(xb, w_in, lb_logits, act="log_forget", col_block=lambda j: 1, n_blocks=1, out_dtype=F32,
                  tm=tm, n_inert=n_inert)
    plain = _proj(xb, w_in, None, act="none", col_block=lambda j: jnp.where(j > 0, j + 3, 2), n_blocks=4,
                  out_dtype=BF16, tm=tm)
    gates = _proj(xb, w_in, b_gate, act="gate", col_block=lambda j: j + 7, n_blocks=2 * d // MIX_W,
                  out_dtype=BF16, tm=tm)
    return silu, log_f, plain, gates


def _hgrn_levels(tt):
    return [m for m in (16, 32, 64, 128, 256, 512) if m <= tt]


def _hgrn_kernel(q_ref, lf_ref, v_ref, og_ref, gain_ref, st0_ref, tri_ref, lvl_ref, o_ref, stf_ref, st_ref,
                 fpad_ref, kpad_ref, vpad_ref, *, tt, lt, hpb):
    t_id = pl.program_id(2)

    @pl.when(t_id == 0)
    def _():
        st_ref[...] = st0_ref[...]

    row = lax.broadcasted_iota(jnp.int32, (tt, HEAD_DIM), 0)
    sub_start = (row & (SUB - 1)) == 0
    zeros = jnp.zeros((SUB, HEAD_DIM), F32)
    lvl = lvl_ref[...]

    for hd in range(hpb):
        lanes = slice(hd * HEAD_DIM, (hd + 1) * HEAD_DIM)
        qb = q_ref[:, lanes]
        q = qb.astype(F32)
        g2 = lf_ref[:, lanes] * LOG2_E
        vb = v_ref[:, lanes]
        v = vb.astype(F32)
        f = jnp.exp2(g2)
        kk = 1.0 - f
        kb = kk.astype(BF16)
        g_hi, g_lo = _split_bf16(g2)

        fpad_ref[hd, 0:SUB, :] = zeros
        kpad_ref[hd, 0:SUB, :] = zeros
        vpad_ref[hd, 0:SUB, :] = zeros
        fpad_ref[hd, SUB:, :] = jnp.where(sub_start, 0.0, f)
        kpad_ref[hd, SUB:, :] = kk
        vpad_ref[hd, SUB:, :] = v
        o = jnp.sum(q * kk, axis=-1, keepdims=True) * v
        u = q
        for d in range(1, SUB):
            u = u * fpad_ref[hd, pl.ds(SUB - d + 1, tt), :]
            k_d = kpad_ref[hd, pl.ds(SUB - d, tt), :]
            v_d = vpad_ref[hd, pl.ds(SUB - d, tt), :]
            o = o + jnp.sum(u * k_d, axis=-1, keepdims=True) * v_d

        for s in range(tt // lt):
            rows = slice(s * lt, (s + 1) * lt)
            b = _dot(tri_ref[...], jnp.concatenate([g_hi[rows], g_lo[rows]], axis=0))

            scores = jnp.zeros((lt, lt), F32)
            for idx, m in enumerate(_hgrn_levels(lt)):
                mid = b.reshape(lt // m, m, HEAD_DIM)[:, m // 2 - 1:m // 2, :]
                b_mid = jnp.broadcast_to(mid, (lt // m, m, HEAD_DIM)).reshape(lt, HEAD_DIM)
                e = jnp.exp2(-jnp.abs(b - b_mid)).astype(BF16)
                scores = jnp.where(lvl == idx, _dot_nt(qb[rows] * e, kb[rows] * e), scores)
            st = st_ref[hd]
            o_s = (o[rows] + _dot(scores.astype(BF16), vb[rows])
                   + _dot_nt(qb[rows] * jnp.exp2(b).astype(BF16), st.astype(BF16)))

            b_end = b[lt - 1:lt, :]
            k_end = kb[rows] * jnp.exp2(b_end - b).astype(BF16)
            st_ref[hd] = st * jnp.exp2(b_end) + _dot_tn(vb[rows], k_end)

            o_s = o_s * lax.rsqrt(jnp.mean(o_s * o_s, axis=-1, keepdims=True) + RMS_EPS)
            o_ref[rows, lanes] = (o_s * gain_ref[:, lanes] * og_ref[rows, lanes].astype(F32)).astype(BF16)

    @pl.when(t_id == pl.num_programs(2) - 1)
    def _():
        stf_ref[...] = st_ref[...]


def _hgrn(q_og, lf, plain, gain, st0, *, tt, hpb):
    bsz, length, _ = lf.shape
    lt = min(tt, HG_LEVEL_TILE)
    levels = _hgrn_levels(lt)
    assert length % tt == 0 and tt % lt == 0 and levels[-1] == lt and SUB * 2 == levels[0] and HEADS % hpb == 0
    r = lax.broadcasted_iota(jnp.int32, (lt, 2 * lt), 0)
    c = lax.broadcasted_iota(jnp.int32, (lt, 2 * lt), 1) % lt
    tri = (c <= r).astype(BF16)
    r = lax.broadcasted_iota(jnp.int32, (lt, lt), 0)
    c = lax.broadcasted_iota(jnp.int32, (lt, lt), 1)
    lvl = jnp.full((lt, lt), -1, jnp.int32)
    for idx, m in reversed(list(enumerate(levels))):
        lvl = jnp.where(r // m == c // m, idx, lvl)
    lvl = jnp.where(jnp.logical_or(r // SUB == c // SUB, c > r), -1, lvl)
    groups = HEADS // hpb
    width = hpb * HEAD_DIM
    tok = lambda off: pl.BlockSpec((None, tt, width), lambda b, h, t: (b, t, h + off))
    const = lambda a: pl.BlockSpec(a.shape, lambda b, h, t: (0, 0))
    return pl.pallas_call(
        functools.partial(_hgrn_kernel, tt=tt, lt=lt, hpb=hpb),
        grid=(bsz, groups, length // tt),
        in_specs=[tok(0), tok(0), tok(0), tok(groups),
                  pl.BlockSpec((1, width), lambda b, h, t: (0, h)),
                  pl.BlockSpec((hpb, HEAD_DIM, HEAD_DIM), lambda b, h, t: (h, 0, 0)),
                  const(tri), const(lvl)],
        out_specs=[tok(0), pl.BlockSpec((None, hpb, HEAD_DIM, HEAD_DIM), lambda b, h, t: (b, h, 0, 0))],
        out_shape=[jax.ShapeDtypeStruct((bsz, length, MIX_W), BF16),
                   jax.ShapeDtypeStruct((bsz, HEADS, HEAD_DIM, HEAD_DIM), F32)],
        scratch_shapes=[pltpu.VMEM((hpb, HEAD_DIM, HEAD_DIM), F32)]
        + [pltpu.VMEM((hpb, tt + SUB, HEAD_DIM), F32)] * 3,
        compiler_params=pltpu.CompilerParams(
            dimension_semantics=("parallel", "parallel", "arbitrary"), vmem_limit_bytes=VMEM_LIMIT),
        name="hgrn2",
    )(q_og, lf, plain, q_og, gain, st0, tri, lvl)


def _sb_kernel(q_ref, k_ref, v_ref, kp_ref, vp_ref, tri_ref, o_ref, acc_ref, rem_ref, *, ts, n_sub):
    qi = pl.program_id(2)
    r_i = lax.broadcasted_iota(jnp.int32, (ts, ts), 0)
    c_i = lax.broadcasted_iota(jnp.int32, (ts, ts), 1)
    causal = c_i < r_i
    prefix_valid = jnp.logical_and(c_i >= PAD, c_i < BLOCK)

    def tile(j):
        start = pl.multiple_of(j * ts, ts)
        return k_ref[pl.ds(start, ts), :], v_ref[pl.ds(start, ts), :]

    def log_weights(q, k_t, mask):
        z = _dot_nt(q, k_t)
        log_beta = jnp.minimum(z, 0.0) - jnp.log(1.0 + jnp.exp(-jnp.abs(z)))
        log_rest = log_beta - z
        if mask is not None:
            log_rest = jnp.where(mask, log_rest, 0.0)
        hi, lo = _split_bf16(log_rest)
        later = _dot(jnp.concatenate([hi, lo], axis=1), tri_ref[...])
        return log_beta + later, jnp.sum(log_rest, axis=-1, keepdims=True)

    def weights(log_w, mask):
        w = jnp.exp(log_w)
        if mask is not None:
            w = jnp.where(mask, w, 0.0)
        return w.astype(BF16)

    def first_visits(sub0_has_prev):
        for s in range(n_sub):
            q = q_ref[s * ts:(s + 1) * ts, :]
            g = qi * n_sub + s
            k_t, v_t = tile(g)
            log_w, rem = log_weights(q, k_t, causal)
            acc = _dot(weights(log_w, causal), v_t)
            if s > 0 or sub0_has_prev:
                k_t, v_t = tile(g - 1)
                log_w, total = log_weights(q, k_t, None)
                acc = acc + _dot(weights(log_w + rem, None), v_t)
                rem = rem + total
            acc_ref[s] = acc
            rem_ref[s] = rem

    @pl.when(qi > 0)
    def _():
        first_visits(True)

    @pl.when(qi == 0)
    def _():
        first_visits(False)

    def finish(s):
        q = q_ref[s * ts:(s + 1) * ts, :]

        def visit(k_t, v_t, mask):
            log_w, total = log_weights(q, k_t, mask)
            rem = rem_ref[s]
            acc_ref[s] += _dot(weights(log_w + rem, mask), v_t)
            rem = rem + total
            rem_ref[s] = rem
            return jnp.max(rem)

        def cond(state):
            j, live = state
            return jnp.logical_and(j >= 0, live > SB_DEAD_LOG)

        def body(state):
            j, _ = state
            return j - 1, visit(*tile(j), None)

        _, live = lax.while_loop(cond, body, (qi * n_sub + s - 2, jnp.max(rem_ref[s])))

        @pl.when(live > SB_DEAD_LOG)
        def _():
            visit(kp_ref[...], vp_ref[...], prefix_valid)

    @pl.when(jnp.max(rem_ref[...]) > SB_DEAD_LOG)
    def _():
        for s in range(n_sub):
            finish(s)

    for s in range(n_sub):
        o_ref[s * ts:(s + 1) * ts, :] = acc_ref[s].astype(BF16)


def _stick_breaking(plain, kp, vp, *, n_sub):
    bsz, length, _ = plain.shape
    ts = SB_TILE
    tq = ts * n_sub
    assert length % tq == 0
    r = lax.broadcasted_iota(jnp.int32, (2 * ts, ts), 0) % ts
    c = lax.broadcasted_iota(jnp.int32, (2 * ts, ts), 1)
    tri = (r > c).astype(BF16)
    seq = lambda off: pl.BlockSpec((None, length, HEAD_DIM), lambda b, h, i: (b, 0, h + off))
    pre = pl.BlockSpec((ts, HEAD_DIM), lambda b, h, i: (0, h))
    return pl.pallas_call(
        functools.partial(_sb_kernel, ts=ts, n_sub=n_sub),
        grid=(bsz, HEADS, length // tq),
        in_specs=[pl.BlockSpec((None, tq, HEAD_DIM), lambda b, h, i: (b, i, h + HEADS)),
                  seq(2 * HEADS), seq(3 * HEADS), pre, pre,
                  pl.BlockSpec((2 * ts, ts), lambda b, h, i: (0, 0))],
        out_specs=pl.BlockSpec((None, tq, HEAD_DIM), lambda b, h, i: (b, i, h)),
        out_shape=jax.ShapeDtypeStruct((bsz, length, MIX_W), BF16),
        scratch_shapes=[pltpu.VMEM((n_sub, ts, HEAD_DIM), F32), pltpu.VMEM((n_sub, ts, 1), F32)],
        compiler_params=pltpu.CompilerParams(
            dimension_semantics=("parallel", "parallel", "arbitrary"), vmem_limit_bytes=VMEM_LIMIT),
        name="stick_breaking",
    )(plain, plain, plain, kp, vp, tri)


def _merge_ln_kernel(h_ref, ohg_ref, osb_ref, ghg_ref, gsb_ref, php_ref, psb_ref, wo_ref, g_ref, b_ref, o_ref, *,
                     n_sub):
    tr = h_ref.shape[0] // n_sub
    for s in range(n_sub):
        rows = slice(s * tr, (s + 1) * tr)
        y = (ghg_ref[rows, :].astype(F32) * _dot(ohg_ref[rows, :], php_ref[...])
             + gsb_ref[rows, :].astype(F32) * _dot(osb_ref[rows, :], psb_ref[...]))
        mix = _dot(y.astype(BF16), wo_ref[...])
        o_ref[rows, :] = _layer_norm(DN_ALPHA * h_ref[rows, :] + mix, g_ref[...], b_ref[...])


def _merge_ln(h, o_hg, o_sb, gates, p_hg, p_sb, w_out, g, b, *, tm):
    n, d = h.shape
    assert n % tm == 0
    row = lambda w: pl.BlockSpec((tm, w), lambda i: (i, 0))
    full = lambda a: pl.BlockSpec(a.shape, lambda i: (0, 0), pipeline_mode=pl.Buffered(1))
    return pl.pallas_call(
        functools.partial(_merge_ln_kernel, n_sub=2 if tm % 512 == 0 else 1),
        grid=(n // tm,),
        in_specs=[row(d), row(MIX_W), row(MIX_W),
                  pl.BlockSpec((tm, d), lambda i: (i, 0)), pl.BlockSpec((tm, d), lambda i: (i, 1)),
                  full(p_hg), full(p_sb), full(w_out), full(g), full(b)],
        out_specs=row(d),
        out_shape=jax.ShapeDtypeStruct((n, d), F32),
        compiler_params=pltpu.CompilerParams(
            dimension_semantics=("parallel",), vmem_limit_bytes=VMEM_LIMIT),
        name="merge_ln",
    )(h, o_hg, o_sb, gates, gates, p_hg, p_sb, w_out, g, b)


CAST_BLOCK_BYTES = 8 * 1024 * 1024


def _cast_kernel(w_ref, *rest):
    o_ref = rest[-1]
    w = w_ref[...]
    if len(rest) == 2:
        w = w * rest[0][...]
    o_ref[...] = w.astype(BF16)


def _to_bf16(w, col_scale=None):
    _, r, c = w.shape
    tr = 16
    while tr * 2 * c * 4 <= CAST_BLOCK_BYTES and r % (tr * 2) == 0:
        tr *= 2
    assert r % tr == 0
    in_specs = [pl.BlockSpec((None, tr, c), lambda i: (0, i, 0))]
    args = [w]
    if col_scale is not None:
        in_specs.append(pl.BlockSpec((1, c), lambda i: (0, 0)))
        args.append(col_scale)
    return pl.pallas_call(
        _cast_kernel,
        grid=(r // tr,),
        in_specs=in_specs,
        out_specs=pl.BlockSpec((tr, c), lambda i: (i, 0)),
        out_shape=jax.ShapeDtypeStruct((r, c), BF16),
        compiler_params=pltpu.CompilerParams(
            dimension_semantics=("parallel",), vmem_limit_bytes=VMEM_LIMIT),
        name="to_bf16",
    )(*args)


def _pick(n, pref):
    t = min(n, pref)
    while n % t:
        t //= 2
    return t


def kernel(x, meta, ln1_g, ln1_b, ffn1_w_gate, ffn1_w_up, ffn1_w_down, w_in, b_gate, hg_lb_logits, hg_norm_g, w_proj_hg, w_proj_sb, w_out, ln2_g, ln2_b, ffn2_w_gate, ffn2_w_up, ffn2_w_down, ln3_g, ln3_b):
    bsz, seq, d = x.shape
    assert ln1_g.shape[0] == DEPTH and hg_lb_logits.shape == (DEPTH + 1, MIX_W) and (2 * d) % MIX_W == 0
    n = bsz * seq
    tm = _pick(n, 1024)
    tf = _pick(ffn1_w_gate.shape[2], 512)
    bf = _to_bf16
    row = lambda a: a[0].reshape(1, -1)

    w1 = (bf(ffn1_w_gate), bf(ffn1_w_up), bf(ffn1_w_down), row(ln1_g), row(ln1_b))
    w2 = (bf(ffn2_w_gate), bf(ffn2_w_up), bf(ffn2_w_down), row(ln3_g), row(ln3_b))
    col = lax.broadcasted_iota(jnp.int32, (1, w_in.shape[2]), 1)
    is_sq = jnp.logical_and(col >= 4 * MIX_W, col < 5 * MIX_W)
    w_in_b = _to_bf16(w_in, jnp.where(is_sq, 1.0 / math.sqrt(HEAD_DIM), 1.0).astype(F32))
    bg = row(b_gate)
    gain = row(hg_norm_g)

    hp = jnp.concatenate([jnp.zeros((PAD, d), x.dtype), meta.astype(x.dtype)], axis=0)
    _, hpb = _ffn_ln(hp, *w1, tm=BLOCK, tf=tf, emit_bf16=True)
    _, lf_p, plain_p, _ = _proj_all(hpb, w_in_b, bg, hg_lb_logits, tm=BLOCK, n_inert=PAD)
    _, st0 = _hgrn(jnp.zeros((1, BLOCK, 2 * MIX_W), BF16), lf_p[None], plain_p[None], gain,
                   jnp.zeros((HEADS, HEAD_DIM, HEAD_DIM), F32), tt=BLOCK, hpb=HG_HEADS_PER_STEP)
    pad_rows = ((0, SB_TILE - BLOCK), (0, 0))
    kp = jnp.pad(plain_p[:, 2 * MIX_W:3 * MIX_W], pad_rows)
    vp = jnp.pad(plain_p[:, 3 * MIX_W:], pad_rows)

    h, hb = _ffn_ln(x.reshape(n, d), *w1, tm=tm, tf=tf, emit_bf16=True)
    q_og, lf, plain, gates = _proj_all(hb, w_in_b, bg, hg_lb_logits, tm=_pick(n, 2048), n_inert=0)
    seq3 = lambda a: a.reshape(bsz, seq, a.shape[-1])
    o_hg, _ = _hgrn(seq3(q_og), seq3(lf), seq3(plain), gain, st0[0], tt=_pick(seq, 256),
                    hpb=HG_HEADS_PER_STEP)
    o_sb = _stick_breaking(seq3(plain), kp, vp, n_sub=_pick(seq // SB_TILE, 8))
    h = _merge_ln(h, o_hg.reshape(n, MIX_W), o_sb.reshape(n, MIX_W), gates, bf(w_proj_hg), bf(w_proj_sb),
                  bf(w_out), row(ln2_g), row(ln2_b), tm=_pick(n, 512))
    (h,) = _ffn_ln(h, *w2, tm=tm, tf=tf, emit_bf16=False)
    return h.reshape(bsz, seq, d)
```

```python
import functools
import math

import jax
import jax.numpy as jnp
from jax import lax
from jax.experimental import pallas as pl
from jax.experimental.pallas import tpu as pltpu

N_META = 16
BLOCK = 128
PAD = BLOCK - N_META
HEADS = 8
HEAD_DIM = 128
MIX_W = HEADS * HEAD_DIM
SUB = 8
HALO = 8
HG_HEADS_PER_STEP = 8
HG_LEVEL_TILE = 128
LOG2_E = math.log2(math.e)
SB_TILE = 256
LN_EPS = 1e-5
RMS_EPS = 1e-6
DEPTH = 1
DN_ALPHA = (2.0 * DEPTH) ** 0.25
SB_DEAD_LOG = -104.0
LN_CHUNK = 512
MXU_COLS = 256
VMEM_LIMIT = 60 * 1024 * 1024

F32 = jnp.float32
BF16 = jnp.bfloat16


def _layer_norm(y, g, b):
    mu = jnp.mean(y, axis=-1, keepdims=True)
    yc = y - mu
    var = jnp.mean(yc * yc, axis=-1, keepdims=True)
    return yc * lax.rsqrt(var + LN_EPS) * g + b


def _dot(a, b):
    return jnp.dot(a, b, preferred_element_type=F32)


def _dot_nt(a, b):
    return lax.dot_general(a, b, (((1,), (1,)), ((), ())), preferred_element_type=F32)


def _dot_tn(a, b):
    return lax.dot_general(a, b, (((0,), (0,)), ((), ())), preferred_element_type=F32)


def _split_bf16(x):
    hi = x.astype(BF16)
    return hi, (x - hi.astype(F32)).astype(BF16)


def _ffn_ln_kernel(x_hbm, wg_ref, wu_ref, wd_ref, g_ref, b_ref, o_ref, *rest, emit_bf16, tm):
    xb_ref, xbuf_ref, sem = rest
    ob_ref = xb_ref if emit_bf16 else None
    i = pl.program_id(0)
    j = pl.program_id(1)
    n_i = pl.num_programs(0)
    n_j = pl.num_programs(1)

    def x_copy(tile):
        return pltpu.make_async_copy(x_hbm.at[pl.ds(pl.multiple_of(tile * tm, tm), tm), :], xbuf_ref, sem)

    @pl.when(jnp.logical_and(i == 0, j == 0))
    def _():
        x_copy(0).start()

    @pl.when(j == 0)
    def _():
        x_copy(i).wait()
        x = xbuf_ref[...]
        xb_ref[...] = x.astype(BF16)
        o_ref[...] = (DN_ALPHA / 0.5) * x

    @pl.when(jnp.logical_and(j == jnp.minimum(1, n_j - 1), i + 1 < n_i))
    def _():
        x_copy(i + 1).start()

    def swiglu_hidden():
        xb = xb_ref[...]
        gate = _dot(xb, wg_ref[...])
        up = _dot(xb, wu_ref[...])
        return (gate * jax.nn.sigmoid(gate) * up).astype(BF16)

    @pl.when(j < n_j - 1)
    def _():
        o_ref[...] += _dot(swiglu_hidden(), wd_ref[...])

    @pl.when(j == n_j - 1)
    def _():
        act = swiglu_hidden()
        d = o_ref.shape[1]
        s1 = jnp.zeros((tm, 1), F32)
        s2 = jnp.zeros((tm, 1), F32)
        for c in range(d // LN_CHUNK):
            cols = slice(c * LN_CHUNK, (c + 1) * LN_CHUNK)
            z = o_ref[:, cols] + _dot(act, wd_ref[:, cols])
            o_ref[:, cols] = z
            s1 = s1 + jnp.sum(z, axis=-1, keepdims=True)
            s2 = s2 + jnp.sum(z * z, axis=-1, keepdims=True)
        mean = s1 * (1.0 / d)
        var = s2 * (1.0 / d) - mean * mean
        y = (o_ref[...] - mean) * lax.rsqrt(var + 4.0 * LN_EPS) * g_ref[...] + b_ref[...]
        o_ref[...] = y
        if ob_ref is not None:
            ob_ref[...] = y.astype(BF16)


def _ffn_ln(x, wg, wu, wd, g, b, *, tm, tf, emit_bf16):
    n, d = x.shape
    dff = wg.shape[1]
    assert n % tm == 0 and dff % tf == 0
    rows = pl.BlockSpec((tm, d), lambda i, j: (i, 0))
    out_shape = [jax.ShapeDtypeStruct((n, d), F32)]
    if emit_bf16:
        out_shape.append(jax.ShapeDtypeStruct((n, d), BF16))
    return pl.pallas_call(
        functools.partial(_ffn_ln_kernel, emit_bf16=emit_bf16, tm=tm),
        grid=(n // tm, dff // tf),
        in_specs=[
            pl.BlockSpec(memory_space=pl.ANY),
            pl.BlockSpec((d, tf), lambda i, j: (0, j)),
            pl.BlockSpec((d, tf), lambda i, j: (0, j)),
            pl.BlockSpec((tf, d), lambda i, j: (j, 0)),
            pl.BlockSpec((1, d), lambda i, j: (0, 0)),
            pl.BlockSpec((1, d), lambda i, j: (0, 0)),
        ],
        out_specs=[rows] * len(out_shape),
        out_shape=out_shape,
        scratch_shapes=([] if emit_bf16 else [pltpu.VMEM((tm, d), BF16)])
        + [pltpu.VMEM((tm, d), F32), pltpu.SemaphoreType.DMA(())],
        compiler_params=pltpu.CompilerParams(
            dimension_semantics=("arbitrary", "arbitrary"), vmem_limit_bytes=VMEM_LIMIT),
        name="ffn_ln",
    )(x, wg, wu, wd, g, b)


def _proj_kernel(x_ref, w_ref, *rest, act, n_inert, tm):
    o_ref = rest[-1]
    x = x_ref[...]
    for c in range(MIX_W // MXU_COLS):
        cols = slice(c * MXU_COLS, (c + 1) * MXU_COLS)
        acc = _dot(x, w_ref[:, cols])
        if act == "silu":
            y = acc * jax.nn.sigmoid(acc)
        elif act == "gate":
            y = jax.nn.sigmoid(acc + rest[0][:, cols])
        elif act == "log_forget":
            l0 = rest[0][0:1, cols]
            l1 = rest[0][1:2, cols]
            m = jnp.maximum(l0, l1)
            e0 = jnp.exp(l0 - m)
            e1 = jnp.exp(l1 - m)
            lb = e0 / (e0 + e1)
            y = jnp.log(lb + (1.0 - lb) * jax.nn.sigmoid(acc))
            if n_inert:
                row = pl.program_id(0) * tm + lax.broadcasted_iota(jnp.int32, y.shape, 0)
                y = jnp.where(row >= n_inert, y, 0.0)
        else:
            y = acc
        o_ref[:, cols] = y.astype(o_ref.dtype)


def _proj(xb, w_in, extra, *, act, col_block, n_blocks, out_dtype, tm, n_inert=0):
    n, d = xb.shape
    assert n % tm == 0
    in_specs = [pl.BlockSpec((tm, d), lambda i, j: (i, 0)),
                pl.BlockSpec((d, MIX_W), lambda i, j: (0, col_block(j)))]
    args = [xb, w_in]
    if act == "gate":
        in_specs.append(pl.BlockSpec((1, MIX_W), lambda i, j: (0, j)))
        args.append(extra)
    elif act == "log_forget":
        in_specs.append(pl.BlockSpec(extra.shape, lambda i, j: (0, 0)))
        args.append(extra)
    return pl.pallas_call(
        functools.partial(_proj_kernel, act=act, n_inert=n_inert, tm=tm),
        grid=(n // tm, n_blocks),
        in_specs=in_specs,
        out_specs=pl.BlockSpec((tm, MIX_W), lambda i, j: (i, j)),
        out_shape=jax.ShapeDtypeStruct((n, n_blocks * MIX_W), out_dtype),
        compiler_params=pltpu.CompilerParams(
            dimension_semantics=("parallel", "arbitrary"), vmem_limit_bytes=VMEM_LIMIT),
        name="proj_" + act,
    )(*args)


def _proj_all(xb, w_in, b_gate, lb_logits, *, tm, n_inert):
    d = xb.shape[1]
    silu = _proj(xb, w_in, None, act="silu", col_block=lambda j: 3 * j, n_blocks=2, out_dtype=BF16, tm=tm)
    log_f = _proj(xb, w_in, lb_logits, act="log_forget", col_block=lambda j: 1, n_blocks=1, out_dtype=F32,
                  tm=tm, n_inert=n_inert)
    plain = _proj(xb, w_in, None, act="none", col_block=lambda j: jnp.where(j > 0, j + 3, 2), n_blocks=4,
                  out_dtype=BF16, tm=tm)
    gates = _proj(xb, w_in, b_gate, act="gate", col_block=lambda j: j + 7, n_blocks=2 * d // MIX_W,
                  out_dtype=BF16, tm=tm)
    return silu, log_f, plain, gates


def _hgrn_levels(tt):
    return [m for m in (2 * SUB, 4 * SUB, 8 * SUB, 16 * SUB, 32 * SUB, 64 * SUB, 128 * SUB) if m <= tt]


def _hgrn_kernel(q_ref, lf_ref, v_ref, og_ref, gain_ref, st0_ref, tri_ref, lvl_ref, o_ref, stf_ref, st_ref,
                 fpad_ref, kpad_ref, vpad_ref, *, tt, lt, hpb):
    t_id = pl.program_id(2)

    @pl.when(t_id == 0)
    def _():
        st_ref[...] = st0_ref[...]

    row = lax.broadcasted_iota(jnp.int32, (tt, HEAD_DIM), 0)
    sub_start = (row & (SUB - 1)) == 0
    zeros = jnp.zeros((HALO, HEAD_DIM), F32)
    lvl = lvl_ref[...]

    for hd in range(hpb):
        lanes = slice(hd * HEAD_DIM, (hd + 1) * HEAD_DIM)
        qb = q_ref[:, lanes]
        q = qb.astype(F32)
        g2 = lf_ref[:, lanes] * LOG2_E
        vb = v_ref[:, lanes]
        v = vb.astype(F32)
        f = jnp.exp2(g2)
        kk = 1.0 - f
        kb = kk.astype(BF16)
        g_hi, g_lo = _split_bf16(g2)

        fpad_ref[hd, 0:HALO, :] = zeros
        kpad_ref[hd, 0:HALO, :] = zeros
        vpad_ref[hd, 0:HALO, :] = zeros
        fpad_ref[hd, HALO:, :] = jnp.where(sub_start, 0.0, f)
        kpad_ref[hd, HALO:, :] = kk
        vpad_ref[hd, HALO:, :] = v
        o = jnp.sum(q * kk, axis=-1, keepdims=True) * v
        u = q
        for d in range(1, SUB):
            u = u * fpad_ref[hd, pl.ds(HALO - d + 1, tt), :]
            k_d = kpad_ref[hd, pl.ds(HALO - d, tt), :]
            v_d = vpad_ref[hd, pl.ds(HALO - d, tt), :]
            o = o + jnp.sum(u * k_d, axis=-1, keepdims=True) * v_d

        for s in range(tt // lt):
            rows = slice(s * lt, (s + 1) * lt)
            b = _dot(tri_ref[...], jnp.concatenate([g_hi[rows], g_lo[rows]], axis=0))

            scores = jnp.zeros((lt, lt), F32)
            for idx, m in enumerate(_hgrn_levels(lt)):
                mid = b.reshape(lt // m, m, HEAD_DIM)[:, m // 2 - 1:m // 2, :]
                b_mid = jnp.broadcast_to(mid, (lt // m, m, HEAD_DIM)).reshape(lt, HEAD_DIM)
                e = jnp.exp2(-jnp.abs(b - b_mid)).astype(BF16)
                scores = jnp.where(lvl == idx, _dot_nt(qb[rows] * e, kb[rows] * e), scores)
            st = st_ref[hd]
            o_s = (o[rows] + _dot(scores.astype(BF16), vb[rows])
                   + _dot_nt(qb[rows] * jnp.exp2(b).astype(BF16), st.astype(BF16)))

            b_end = b[lt - 1:lt, :]
            k_end = kb[rows] * jnp.exp2(b_end - b).astype(BF16)
            st_ref[hd] = st * jnp.exp2(b_end) + _dot_tn(vb[rows], k_end)

            o_s = o_s * lax.rsqrt(jnp.mean(o_s * o_s, axis=-1, keepdims=True) + RMS_EPS)
            o_ref[rows, lanes] = (o_s * gain_ref[:, lanes] * og_ref[rows, lanes].astype(F32)).astype(BF16)

    @pl.when(t_id == pl.num_programs(2) - 1)
    def _():
        stf_ref[...] = st_ref[...]


def _hgrn(q_og, lf, plain, gain, st0, *, tt, hpb):
    bsz, length, _ = lf.shape
    lt = min(tt, HG_LEVEL_TILE)
    levels = _hgrn_levels(lt)
    assert length % tt == 0 and tt % lt == 0 and levels[-1] == lt and SUB * 2 == levels[0] and HEADS % hpb == 0
    r = lax.broadcasted_iota(jnp.int32, (lt, 2 * lt), 0)
    c = lax.broadcasted_iota(jnp.int32, (lt, 2 * lt), 1) % lt
    tri = (c <= r).astype(BF16)
    r = lax.broadcasted_iota(jnp.int32, (lt, lt), 0)
    c = lax.broadcasted_iota(jnp.int32, (lt, lt), 1)
    lvl = jnp.full((lt, lt), -1, jnp.int32)
    for idx, m in reversed(list(enumerate(levels))):
        lvl = jnp.where(r // m == c // m, idx, lvl)
    lvl = jnp.where(jnp.logical_or(r // SUB == c // SUB, c > r), -1, lvl)
    groups = HEADS // hpb
    width = hpb * HEAD_DIM
    tok = lambda off: pl.BlockSpec((None, tt, width), lambda b, h, t: (b, t, h + off))
    const = lambda a: pl.BlockSpec(a.shape, lambda b, h, t: (0, 0))
    return pl.pallas_call(
        functools.partial(_hgrn_kernel, tt=tt, lt=lt, hpb=hpb),
        grid=(bsz, groups, length // tt),
        in_specs=[tok(0), tok(0), tok(0), tok(groups),
                  pl.BlockSpec((1, width), lambda b, h, t: (0, h)),
                  pl.BlockSpec((hpb, HEAD_DIM, HEAD_DIM), lambda b, h, t: (h, 0, 0)),
                  const(tri), const(lvl)],
        out_specs=[tok(0), pl.BlockSpec((None, hpb, HEAD_DIM, HEAD_DIM), lambda b, h, t: (b, h, 0, 0))],
        out_shape=[jax.ShapeDtypeStruct((bsz, length, MIX_W), BF16),
                   jax.ShapeDtypeStruct((bsz, HEADS, HEAD_DIM, HEAD_DIM), F32)],
        scratch_shapes=[pltpu.VMEM((hpb, HEAD_DIM, HEAD_DIM), F32)]
        + [pltpu.VMEM((hpb, tt + HALO, HEAD_DIM), F32)] * 3,
        compiler_params=pltpu.CompilerParams(
            dimension_semantics=("parallel", "parallel", "arbitrary"), vmem_limit_bytes=VMEM_LIMIT),
        name="hgrn2",
    )(q_og, lf, plain, q_og, gain, st0, tri, lvl)


def _sb_kernel(q_ref, k_ref, v_ref, kp_ref, vp_ref, tri_ref, o_ref, acc_ref, rem_ref, *, ts, n_sub):
    qi = pl.program_id(2)
    r_i = lax.broadcasted_iota(jnp.int32, (ts, ts), 0)
    c_i = lax.broadcasted_iota(jnp.int32, (ts, ts), 1)
    causal = c_i < r_i
    prefix_valid = jnp.logical_and(c_i >= PAD, c_i < BLOCK)

    def tile(j):
        start = pl.multiple_of(j * ts, ts)
        return k_ref[pl.ds(start, ts), :], v_ref[pl.ds(start, ts), :]

    def log_weights(q, k_t, mask):
        z = _dot_nt(q, k_t)
        log_beta = jnp.minimum(z, 0.0) - jnp.log(1.0 + jnp.exp(-jnp.abs(z)))
        log_rest = log_beta - z
        if mask is not None:
            log_rest = jnp.where(mask, log_rest, 0.0)
        hi, lo = _split_bf16(log_rest)
        later = _dot(jnp.concatenate([hi, lo], axis=1), tri_ref[...])
        return log_beta + later, jnp.sum(log_rest, axis=-1, keepdims=True)

    def weights(log_w, mask):
        w = jnp.exp(log_w)
        if mask is not None:
            w = jnp.where(mask, w, 0.0)
        return w.astype(BF16)

    def first_visits(sub0_has_prev):
        for s in range(n_sub):
            q = q_ref[s * ts:(s + 1) * ts, :]
            g = qi * n_sub + s
            k_t, v_t = tile(g)
            log_w, rem = log_weights(q, k_t, causal)
            acc = _dot(weights(log_w, causal), v_t)
            if s > 0 or sub0_has_prev:
                k_t, v_t = tile(g - 1)
                log_w, total = log_weights(q, k_t, None)
                acc = acc + _dot(weights(log_w + rem, None), v_t)
                rem = rem + total
            acc_ref[s] = acc
            rem_ref[s] = rem

    @pl.when(qi > 0)
    def _():
        first_visits(True)

    @pl.when(qi == 0)
    def _():
        first_visits(False)

    def finish(s):
        q = q_ref[s * ts:(s + 1) * ts, :]

        def visit(k_t, v_t, mask):
            log_w, total = log_weights(q, k_t, mask)
            rem = rem_ref[s]
            acc_ref[s] += _dot(weights(log_w + rem, mask), v_t)
            rem = rem + total
            rem_ref[s] = rem
            return jnp.max(rem)

        def cond(state):
            j, live = state
            return jnp.logical_and(j >= 0, live > SB_DEAD_LOG)

        def body(state):
            j, _ = state
            return j - 1, visit(*tile(j), None)

        _, live = lax.while_loop(cond, body, (qi * n_sub + s - 2, jnp.max(rem_ref[s])))

        @pl.when(live > SB_DEAD_LOG)
        def _():
            visit(kp_ref[...], vp_ref[...], prefix_valid)

    @pl.when(jnp.max(rem_ref[...]) > SB_DEAD_LOG)
    def _():
        for s in range(n_sub):
            finish(s)

    for s in range(n_sub):
        o_ref[s * ts:(s + 1) * ts, :] = acc_ref[s].astype(BF16)


def _stick_breaking(plain, kp, vp, *, n_sub):
    bsz, length, _ = plain.shape
    ts = SB_TILE
    tq = ts * n_sub
    assert length % tq == 0
    r = lax.broadcasted_iota(jnp.int32, (2 * ts, ts), 0) % ts
    c = lax.broadcasted_iota(jnp.int32, (2 * ts, ts), 1)
    tri = (r > c).astype(BF16)
    seq = lambda off: pl.BlockSpec((None, length, HEAD_DIM), lambda b, h, i: (b, 0, h + off))
    pre = pl.BlockSpec((ts, HEAD_DIM), lambda b, h, i: (0, h))
    return pl.pallas_call(
        functools.partial(_sb_kernel, ts=ts, n_sub=n_sub),
        grid=(bsz, HEADS, length // tq),
        in_specs=[pl.BlockSpec((None, tq, HEAD_DIM), lambda b, h, i: (b, i, h + HEADS)),
                  seq(2 * HEADS), seq(3 * HEADS), pre, pre,
                  pl.BlockSpec((2 * ts, ts), lambda b, h, i: (0, 0))],
        out_specs=pl.BlockSpec((None, tq, HEAD_DIM), lambda b, h, i: (b, i, h)),
        out_shape=jax.ShapeDtypeStruct((bsz, length, MIX_W), BF16),
        scratch_shapes=[pltpu.VMEM((n_sub, ts, HEAD_DIM), F32), pltpu.VMEM((n_sub, ts, 1), F32)],
        compiler_params=pltpu.CompilerParams(
            dimension_semantics=("parallel", "parallel", "arbitrary"), vmem_limit_bytes=VMEM_LIMIT),
        name="stick_breaking",
    )(plain, plain, plain, kp, vp, tri)


def _merge_ln_kernel(h_ref, ohg_ref, osb_ref, ghg_ref, gsb_ref, php_ref, psb_ref, wo_ref, g_ref, b_ref, o_ref):
    y = (ghg_ref[...].astype(F32) * _dot(ohg_ref[...], php_ref[...])
         + gsb_ref[...].astype(F32) * _dot(osb_ref[...], psb_ref[...])).astype(BF16)
    tm, d = o_ref.shape
    s1 = jnp.zeros((tm, 1), F32)
    s2 = jnp.zeros((tm, 1), F32)
    for c in range(d // LN_CHUNK):
        cols = slice(c * LN_CHUNK, (c + 1) * LN_CHUNK)
        z = DN_ALPHA * h_ref[:, cols] + _dot(y, wo_ref[:, cols])
        o_ref[:, cols] = z
        s1 = s1 + jnp.sum(z, axis=-1, keepdims=True)
        s2 = s2 + jnp.sum(z * z, axis=-1, keepdims=True)
    mean = s1 * (1.0 / d)
    var = s2 * (1.0 / d) - mean * mean
    o_ref[...] = (o_ref[...] - mean) * lax.rsqrt(var + LN_EPS) * g_ref[...] + b_ref[...]


def _merge_ln(h, o_hg, o_sb, gates, p_hg, p_sb, w_out, g, b, *, tm):
    n, d = h.shape
    assert n % tm == 0
    row = lambda w: pl.BlockSpec((tm, w), lambda i: (i, 0))
    full = lambda a: pl.BlockSpec(a.shape, lambda i: (0, 0), pipeline_mode=pl.Buffered(1))
    return pl.pallas_call(
        _merge_ln_kernel,
        grid=(n // tm,),
        in_specs=[row(d), row(MIX_W), row(MIX_W),
                  pl.BlockSpec((tm, d), lambda i: (i, 0)), pl.BlockSpec((tm, d), lambda i: (i, 1)),
                  full(p_hg), full(p_sb), full(w_out), full(g), full(b)],
        out_specs=row(d),
        out_shape=jax.ShapeDtypeStruct((n, d), F32),
        compiler_params=pltpu.CompilerParams(
            dimension_semantics=("parallel",), vmem_limit_bytes=VMEM_LIMIT),
        name="merge_ln",
    )(h, o_hg, o_sb, gates, gates, p_hg, p_sb, w_out, g, b)


CAST_BLOCK_BYTES = 8 * 1024 * 1024


def _cast_kernel(w_ref, *rest):
    o_ref = rest[-1]
    w = w_ref[...]
    if len(rest) == 2:
        w = w * rest[0][...]
    o_ref[...] = w.astype(BF16)


def _to_bf16(w, col_scale=None):
    _, r, c = w.shape
    tr = 16
    while tr * 2 * c * 4 <= CAST_BLOCK_BYTES and r % (tr * 2) == 0:
        tr *= 2
    assert r % tr == 0
    in_specs = [pl.BlockSpec((None, tr, c), lambda i: (0, i, 0))]
    args = [w]
    if col_scale is not None:
        in_specs.append(pl.BlockSpec((1, c), lambda i: (0, 0)))
        args.append(col_scale)
    return pl.pallas_call(
        _cast_kernel,
        grid=(r // tr,),
        in_specs=in_specs,
        out_specs=pl.BlockSpec((tr, c), lambda i: (i, 0)),
        out_shape=jax.ShapeDtypeStruct((r, c), BF16),
        compiler_params=pltpu.CompilerParams(
            dimension_semantics=("parallel",), vmem_limit_bytes=VMEM_LIMIT),
        name="to_bf16",
    )(*args)


def _pick(n, pref):
    t = min(n, pref)
    while n % t:
        t //= 2
    return t


def kernel(x, meta, ln1_g, ln1_b, ffn1_w_gate, ffn1_w_up, ffn1_w_down, w_in, b_gate, hg_lb_logits, hg_norm_g, w_proj_hg, w_proj_sb, w_out, ln2_g, ln2_b, ffn2_w_gate, ffn2_w_up, ffn2_w_down, ln3_g, ln3_b):
    bsz, seq, d = x.shape
    assert ln1_g.shape[0] == DEPTH and hg_lb_logits.shape == (DEPTH + 1, MIX_W) and (2 * d) % MIX_W == 0
    n = bsz * seq
    tm = _pick(n, 1024)
    tf = _pick(ffn1_w_gate.shape[2], 512)
    bf = _to_bf16
    row = lambda a: a[0].reshape(1, -1)

    w1 = (bf(ffn1_w_gate), bf(ffn1_w_up), bf(ffn1_w_down), row(ln1_g), row(ln1_b))
    w2 = (bf(ffn2_w_gate), bf(ffn2_w_up), bf(ffn2_w_down), row(ln3_g), row(ln3_b))
    col = lax.broadcasted_iota(jnp.int32, (1, w_in.shape[2]), 1)
    is_sq = jnp.logical_and(col >= 4 * MIX_W, col < 5 * MIX_W)
    w_in_b = _to_bf16(w_in, jnp.where(is_sq, 1.0 / math.sqrt(HEAD_DIM), 1.0).astype(F32))
    bg = row(b_gate)
    gain = row(hg_norm_g)

    hp = jnp.concatenate([jnp.zeros((PAD, d), x.dtype), meta.astype(x.dtype)], axis=0)
    _, hpb = _ffn_ln(hp, *w1, tm=BLOCK, tf=tf, emit_bf16=True)
    _, lf_p, plain_p, _ = _proj_all(hpb, w_in_b, bg, hg_lb_logits, tm=BLOCK, n_inert=PAD)
    _, st0 = _hgrn(jnp.zeros((1, BLOCK, 2 * MIX_W), BF16), lf_p[None], plain_p[None], gain,
                   jnp.zeros((HEADS, HEAD_DIM, HEAD_DIM), F32), tt=BLOCK, hpb=HG_HEADS_PER_STEP)
    pad_rows = ((0, SB_TILE - BLOCK), (0, 0))
    kp = jnp.pad(plain_p[:, 2 * MIX_W:3 * MIX_W], pad_rows)
    vp = jnp.pad(plain_p[:, 3 * MIX_W:], pad_rows)

    h, hb = _ffn_ln(x.reshape(n, d), *w1, tm=tm, tf=tf, emit_bf16=True)
    q_og, lf, plain, gates = _proj_all(hb, w_in_b, bg, hg_lb_logits, tm=_pick(n, 2048), n_inert=0)
    seq3 = lambda a: a.reshape(bsz, seq, a.shape[-1])
    o_hg, _ = _hgrn(seq3(q_og), seq3(lf), seq3(plain), gain, st0[0], tt=_pick(seq, 256),
                    hpb=HG_HEADS_PER_STEP)
    o_sb = _stick_breaking(seq3(plain), kp, vp, n_sub=_pick(seq // SB_TILE, 8))
    h = _merge_ln(h, o_hg.reshape(n, MIX_W), o_sb.reshape(n, MIX_W), gates, bf(w_proj_hg), bf(w_proj_sb),
                  bf(w_out), row(ln2_g), row(ln2_b), tm=_pick(n, 512))
    (h,) = _ffn_ln(h, *w2, tm=tm, tf=tf, emit_bf16=False)
    return h.reshape(bsz, seq, d)
```

```python
import functools
import math

import jax
import jax.numpy as jnp
from jax import lax
from jax.experimental import pallas as pl
from jax.experimental.pallas import tpu as pltpu

N_META = 16
BLOCK = 128
PAD = BLOCK - N_META
HEADS = 8
HEAD_DIM = 128
MIX_W = HEADS * HEAD_DIM
SUB = 8
HALO = 8
HG_HEADS_PER_STEP = 8
HG_LEVEL_TILE = 128
LOG2_E = math.log2(math.e)
SB_TILE = 256
LN_EPS = 1e-5
RMS_EPS = 1e-6
DEPTH = 1
DN_ALPHA = (2.0 * DEPTH) ** 0.25
SB_DEAD_LOG = -104.0
LN_CHUNK = 512
MXU_COLS = 256
VMEM_LIMIT = 60 * 1024 * 1024

F32 = jnp.float32
BF16 = jnp.bfloat16


def _layer_norm(y, g, b):
    mu = jnp.mean(y, axis=-1, keepdims=True)
    yc = y - mu
    var = jnp.mean(yc * yc, axis=-1, keepdims=True)
    return yc * lax.rsqrt(var + LN_EPS) * g + b


def _dot(a, b):
    return jnp.dot(a, b, preferred_element_type=F32)


def _dot_nt(a, b):
    return lax.dot_general(a, b, (((1,), (1,)), ((), ())), preferred_element_type=F32)


def _dot_tn(a, b):
    return lax.dot_general(a, b, (((0,), (0,)), ((), ())), preferred_element_type=F32)


def _split_bf16(x):
    hi = x.astype(BF16)
    return hi, (x - hi.astype(F32)).astype(BF16)


def _ffn_ln_kernel(x_hbm, wg_ref, wu_ref, wd_ref, g_ref, b_ref, o_ref, *rest, emit_bf16, tm, n_j):
    xb_ref, xbuf_ref, sem = rest
    ob_ref = xb_ref if emit_bf16 else None
    i = pl.program_id(0)
    j = pl.program_id(1)
    n_i = pl.num_programs(0)

    def x_copy(tile):
        return pltpu.make_async_copy(x_hbm.at[pl.ds(pl.multiple_of(tile * tm, tm), tm), :], xbuf_ref, sem)

    @pl.when(jnp.logical_and(i == 0, j == 0))
    def _():
        x_copy(0).start()

    def swiglu_hidden():
        xb = xb_ref[...]
        gate = _dot(xb, wg_ref[...])
        up = _dot(xb, wu_ref[...])
        return (gate * jax.nn.sigmoid(gate) * up).astype(BF16)

    @pl.when(j == 0)
    def _():
        x_copy(i).wait()
        xb_ref[...] = xbuf_ref[...].astype(BF16)
        if n_j == 1:
            o_ref[...] = (DN_ALPHA / 0.5) * xbuf_ref[...]

    if n_j > 1:
        @pl.when(j == 0)
        def _():
            o_ref[...] = (DN_ALPHA / 0.5) * xbuf_ref[...] + _dot(swiglu_hidden(), wd_ref[...])

    @pl.when(jnp.logical_and(j == min(1, n_j - 1), i + 1 < n_i))
    def _():
        x_copy(i + 1).start()

    @pl.when(jnp.logical_and(j > 0, j < n_j - 1))
    def _():
        o_ref[...] += _dot(swiglu_hidden(), wd_ref[...])

    @pl.when(j == n_j - 1)
    def _():
        act = swiglu_hidden()
        d = o_ref.shape[1]
        s1 = jnp.zeros((tm, 1), F32)
        s2 = jnp.zeros((tm, 1), F32)
        for c in range(d // LN_CHUNK):
            cols = slice(c * LN_CHUNK, (c + 1) * LN_CHUNK)
            z = o_ref[:, cols] + _dot(act, wd_ref[:, cols])
            o_ref[:, cols] = z
            s1 = s1 + jnp.sum(z, axis=-1, keepdims=True)
            s2 = s2 + jnp.sum(z * z, axis=-1, keepdims=True)
        mean = s1 * (1.0 / d)
        var = s2 * (1.0 / d) - mean * mean
        y = (o_ref[...] - mean) * lax.rsqrt(var + 4.0 * LN_EPS) * g_ref[...] + b_ref[...]
        o_ref[...] = y
        if ob_ref is not None:
            ob_ref[...] = y.astype(BF16)


def _ffn_ln(x, wg, wu, wd, g, b, *, tm, tf, emit_bf16):
    n, d = x.shape
    dff = wg.shape[1]
    assert n % tm == 0 and dff % tf == 0
    rows = pl.BlockSpec((tm, d), lambda i, j: (i, 0))
    out_shape = [jax.ShapeDtypeStruct((n, d), F32)]
    if emit_bf16:
        out_shape.append(jax.ShapeDtypeStruct((n, d), BF16))
    return pl.pallas_call(
        functools.partial(_ffn_ln_kernel, emit_bf16=emit_bf16, tm=tm, n_j=dff // tf),
        grid=(n // tm, dff // tf),
        in_specs=[
            pl.BlockSpec(memory_space=pl.ANY),
            pl.BlockSpec((d, tf), lambda i, j: (0, j)),
            pl.BlockSpec((d, tf), lambda i, j: (0, j)),
            pl.BlockSpec((tf, d), lambda i, j: (j, 0)),
            pl.BlockSpec((1, d), lambda i, j: (0, 0)),
            pl.BlockSpec((1, d), lambda i, j: (0, 0)),
        ],
        out_specs=[rows] * len(out_shape),
        out_shape=out_shape,
        scratch_shapes=([] if emit_bf16 else [pltpu.VMEM((tm, d), BF16)])
        + [pltpu.VMEM((tm, d), F32), pltpu.SemaphoreType.DMA(())],
        compiler_params=pltpu.CompilerParams(
            dimension_semantics=("arbitrary", "arbitrary"), vmem_limit_bytes=VMEM_LIMIT),
        name="ffn_ln",
    )(x, wg, wu, wd, g, b)


def _proj_kernel(x_ref, w_ref, *rest, act, n_inert, tm):
    o_ref = rest[-1]
    x = x_ref[...]
    for c in range(MIX_W // MXU_COLS):
        cols = slice(c * MXU_COLS, (c + 1) * MXU_COLS)
        acc = _dot(x, w_ref[:, cols])
        if act == "silu":
            y = acc * jax.nn.sigmoid(acc)
        elif act == "gate":
            y = jax.nn.sigmoid(acc + rest[0][:, cols])
        elif act == "log_forget":
            l0 = rest[0][0:1, cols]
            l1 = rest[0][1:2, cols]
            m = jnp.maximum(l0, l1)
            e0 = jnp.exp(l0 - m)
            e1 = jnp.exp(l1 - m)
            lb = e0 / (e0 + e1)
            y = jnp.log(lb + (1.0 - lb) * jax.nn.sigmoid(acc))
            if n_inert:
                row = pl.program_id(0) * tm + lax.broadcasted_iota(jnp.int32, y.shape, 0)
                y = jnp.where(row >= n_inert, y, 0.0)
        else:
            y = acc
        o_ref[:, cols] = y.astype(o_ref.dtype)


def _proj(xb, w_in, extra, *, act, col_block, n_blocks, out_dtype, tm, n_inert=0):
    n, d = xb.shape
    assert n % tm == 0
    in_specs = [pl.BlockSpec((tm, d), lambda i, j: (i, 0)),
                pl.BlockSpec((d, MIX_W), lambda i, j: (0, col_block(j)))]
    args = [xb, w_in]
    if act == "gate":
        in_specs.append(pl.BlockSpec((1, MIX_W), lambda i, j: (0, j)))
        args.append(extra)
    elif act == "log_forget":
        in_specs.append(pl.BlockSpec(extra.shape, lambda i, j: (0, 0)))
        args.append(extra)
    return pl.pallas_call(
        functools.partial(_proj_kernel, act=act, n_inert=n_inert, tm=tm),
        grid=(n // tm, n_blocks),
        in_specs=in_specs,
        out_specs=pl.BlockSpec((tm, MIX_W), lambda i, j: (i, j)),
        out_shape=jax.ShapeDtypeStruct((n, n_blocks * MIX_W), out_dtype),
        compiler_params=pltpu.CompilerParams(
            dimension_semantics=("parallel", "arbitrary"), vmem_limit_bytes=VMEM_LIMIT),
        name="proj_" + act,
    )(*args)


def _proj_all(xb, w_in, b_gate, lb_logits, *, tm, n_inert):
    d = xb.shape[1]
    silu = _proj(xb, w_in, None, act="silu", col_block=lambda j: 3 * j, n_blocks=2, out_dtype=BF16, tm=tm)
    log_f = _proj(xb, w_in, lb_logits, act="log_forget", col_block=lambda j: 1, n_blocks=1, out_dtype=F32,
                  tm=tm, n_inert=n_inert)
    plain = _proj(xb, w_in, None, act="none", col_block=lambda j: jnp.where(j > 0, j + 3, 2), n_blocks=4,
                  out_dtype=BF16, tm=tm)
    gates = _proj(xb, w_in, b_gate, act="gate", col_block=lambda j: j + 7, n_blocks=2 * d // MIX_W,
                  out_dtype=BF16, tm=tm)
    return silu, log_f, plain, gates


def _hgrn_levels(tt):
    return [m for m in (2 * SUB, 4 * SUB, 8 * SUB, 16 * SUB, 32 * SUB, 64 * SUB, 128 * SUB) if m <= tt]


def _hgrn_kernel(q_ref, lf_ref, v_ref, og_ref, gain_ref, st0_ref, tri_ref, lvl_ref, o_ref, stf_ref, st_ref,
                 fpad_ref, kpad_ref, vpad_ref, *, tt, lt, hpb):
    t_id = pl.program_id(2)

    @pl.when(t_id == 0)
    def _():
        st_ref[...] = st0_ref[...]

    row = lax.broadcasted_iota(jnp.int32, (tt, HEAD_DIM), 0)
    sub_start = (row & (SUB - 1)) == 0
    zeros = jnp.zeros((HALO, HEAD_DIM), F32)
    lvl = lvl_ref[...]

    for hd in range(hpb):
        lanes = slice(hd * HEAD_DIM, (hd + 1) * HEAD_DIM)
        qb = q_ref[:, lanes]
        q = qb.astype(F32)
        g2 = lf_ref[:, lanes] * LOG2_E
        vb = v_ref[:, lanes]
        v = vb.astype(F32)
        f = jnp.exp2(g2)
        kk = 1.0 - f
        kb = kk.astype(BF16)
        g_hi, g_lo = _split_bf16(g2)

        fpad_ref[hd, 0:HALO, :] = zeros
        kpad_ref[hd, 0:HALO, :] = zeros
        vpad_ref[hd, 0:HALO, :] = zeros
        fpad_ref[hd, HALO:, :] = jnp.where(sub_start, 0.0, f)
        kpad_ref[hd, HALO:, :] = kk
        vpad_ref[hd, HALO:, :] = v
        o = jnp.sum(q * kk, axis=-1, keepdims=True) * v
        u = q
        for d in range(1, SUB):
            u = u * fpad_ref[hd, pl.ds(HALO - d + 1, tt), :]
            k_d = kpad_ref[hd, pl.ds(HALO - d, tt), :]
            v_d = vpad_ref[hd, pl.ds(HALO - d, tt), :]
            o = o + jnp.sum(u * k_d, axis=-1, keepdims=True) * v_d

        for s in range(tt // lt):
            rows = slice(s * lt, (s + 1) * lt)
            b = _dot(tri_ref[...], jnp.concatenate([g_hi[rows], g_lo[rows]], axis=0))

            scores = jnp.zeros((lt, lt), F32)
            for idx, m in enumerate(_hgrn_levels(lt)):
                mid = b.reshape(lt // m, m, HEAD_DIM)[:, m // 2 - 1:m // 2, :]
                b_mid = jnp.broadcast_to(mid, (lt // m, m, HEAD_DIM)).reshape(lt, HEAD_DIM)
                e = jnp.exp2(-jnp.abs(b - b_mid)).astype(BF16)
                scores = jnp.where(lvl == idx, _dot_nt(qb[rows] * e, kb[rows] * e), scores)
            st = st_ref[hd]
            o_s = (o[rows] + _dot(scores.astype(BF16), vb[rows])
                   + _dot_nt(qb[rows] * jnp.exp2(b).astype(BF16), st.astype(BF16)))

            b_end = b[lt - 1:lt, :]
            k_end = kb[rows] * jnp.exp2(b_end - b).astype(BF16)
            st_ref[hd] = st * jnp.exp2(b_end) + _dot_tn(vb[rows], k_end)

            o_s = o_s * lax.rsqrt(jnp.mean(o_s * o_s, axis=-1, keepdims=True) + RMS_EPS)
            o_ref[rows, lanes] = (o_s * gain_ref[:, lanes] * og_ref[rows, lanes].astype(F32)).astype(BF16)

    @pl.when(t_id == pl.num_programs(2) - 1)
    def _():
        stf_ref[...] = st_ref[...]


def _hgrn(q_og, lf, plain, gain, st0, *, tt, hpb):
    bsz, length, _ = lf.shape
    lt = min(tt, HG_LEVEL_TILE)
    levels = _hgrn_levels(lt)
    assert length % tt == 0 and tt % lt == 0 and levels[-1] == lt and SUB * 2 == levels[0] and HEADS % hpb == 0
    r = lax.broadcasted_iota(jnp.int32, (lt, 2 * lt), 0)
    c = lax.broadcasted_iota(jnp.int32, (lt, 2 * lt), 1) % lt
    tri = (c <= r).astype(BF16)
    r = lax.broadcasted_iota(jnp.int32, (lt, lt), 0)
    c = lax.broadcasted_iota(jnp.int32, (lt, lt), 1)
    lvl = jnp.full((lt, lt), -1, jnp.int32)
    for idx, m in reversed(list(enumerate(levels))):
        lvl = jnp.where(r // m == c // m, idx, lvl)
    lvl = jnp.where(jnp.logical_or(r // SUB == c // SUB, c > r), -1, lvl)
    groups = HEADS // hpb
    width = hpb * HEAD_DIM
    tok = lambda off: pl.BlockSpec((None, tt, width), lambda b, h, t: (b, t, h + off))
    const = lambda a: pl.BlockSpec(a.shape, lambda b, h, t: (0, 0))
    return pl.pallas_call(
        functools.partial(_hgrn_kernel, tt=tt, lt=lt, hpb=hpb),
        grid=(bsz, groups, length // tt),
        in_specs=[tok(0), tok(0), tok(0), tok(groups),
                  pl.BlockSpec((1, width), lambda b, h, t: (0, h)),
                  pl.BlockSpec((hpb, HEAD_DIM, HEAD_DIM), lambda b, h, t: (h, 0, 0)),
                  const(tri), const(lvl)],
        out_specs=[tok(0), pl.BlockSpec((None, hpb, HEAD_DIM, HEAD_DIM), lambda b, h, t: (b, h, 0, 0))],
        out_shape=[jax.ShapeDtypeStruct((bsz, length, MIX_W), BF16),
                   jax.ShapeDtypeStruct((bsz, HEADS, HEAD_DIM, HEAD_DIM), F32)],
        scratch_shapes=[pltpu.VMEM((hpb, HEAD_DIM, HEAD_DIM), F32)]
        + [pltpu.VMEM((hpb, tt + HALO, HEAD_DIM), F32)] * 3,
        compiler_params=pltpu.CompilerParams(
            dimension_semantics=("parallel", "parallel", "arbitrary"), vmem_limit_bytes=VMEM_LIMIT),
        name="hgrn2",
    )(q_og, lf, plain, q_og, gain, st0, tri, lvl)


def _sb_kernel(q_ref, k_ref, v_ref, kp_ref, vp_ref, tri_ref, o_ref, acc_ref, rem_ref, *, ts, n_sub):
    qi = pl.program_id(2)
    r_i = lax.broadcasted_iota(jnp.int32, (ts, ts), 0)
    c_i = lax.broadcasted_iota(jnp.int32, (ts, ts), 1)
    causal = c_i < r_i
    prefix_valid = jnp.logical_and(c_i >= PAD, c_i < BLOCK)

    def tile(j):
        start = pl.multiple_of(j * ts, ts)
        return k_ref[pl.ds(start, ts), :], v_ref[pl.ds(start, ts), :]

    def log_weights(q, k_t, mask):
        z = _dot_nt(q, k_t)
        log_beta = jnp.minimum(z, 0.0) - jnp.log(1.0 + jnp.exp(-jnp.abs(z)))
        log_rest = log_beta - z
        if mask is not None:
            log_rest = jnp.where(mask, log_rest, 0.0)
        hi, lo = _split_bf16(log_rest)
        later = _dot(jnp.concatenate([hi, lo], axis=1), tri_ref[...])
        return log_beta + later, jnp.sum(log_rest, axis=-1, keepdims=True)

    def weights(log_w, mask):
        w = jnp.exp(log_w)
        if mask is not None:
            w = jnp.where(mask, w, 0.0)
        return w.astype(BF16)

    def first_visits(sub0_has_prev):
        for s in range(n_sub):
            q = q_ref[s * ts:(s + 1) * ts, :]
            g = qi * n_sub + s
            k_t, v_t = tile(g)
            log_w, rem = log_weights(q, k_t, causal)
            acc = _dot(weights(log_w, causal), v_t)
            if s > 0 or sub0_has_prev:
                k_t, v_t = tile(g - 1)
                log_w, total = log_weights(q, k_t, None)
                acc = acc + _dot(weights(log_w + rem, None), v_t)
                rem = rem + total
            acc_ref[s] = acc
            rem_ref[s] = rem

    @pl.when(qi > 0)
    def _():
        first_visits(True)

    @pl.when(qi == 0)
    def _():
        first_visits(False)

    def finish(s):
        q = q_ref[s * ts:(s + 1) * ts, :]

        def visit(k_t, v_t, mask):
            log_w, total = log_weights(q, k_t, mask)
            rem = rem_ref[s]
            acc_ref[s] += _dot(weights(log_w + rem, mask), v_t)
            rem = rem + total
            rem_ref[s] = rem
            return jnp.max(rem)

        def cond(state):
            j, live = state
            return jnp.logical_and(j >= 0, live > SB_DEAD_LOG)

        def body(state):
            j, _ = state
            return j - 1, visit(*tile(j), None)

        _, live = lax.while_loop(cond, body, (qi * n_sub + s - 2, jnp.max(rem_ref[s])))

        @pl.when(live > SB_DEAD_LOG)
        def _():
            visit(kp_ref[...], vp_ref[...], prefix_valid)

    @pl.when(jnp.max(rem_ref[...]) > SB_DEAD_LOG)
    def _():
        for s in range(n_sub):
            finish(s)

    for s in range(n_sub):
        o_ref[s * ts:(s + 1) * ts, :] = acc_ref[s].astype(BF16)


def _stick_breaking(plain, kp, vp, *, n_sub):
    bsz, length, _ = plain.shape
    ts = SB_TILE
    tq = ts * n_sub
    assert length % tq == 0
    r = lax.broadcasted_iota(jnp.int32, (2 * ts, ts), 0) % ts
    c = lax.broadcasted_iota(jnp.int32, (2 * ts, ts), 1)
    tri = (r > c).astype(BF16)
    seq = lambda off: pl.BlockSpec((None, length, HEAD_DIM), lambda b, h, i: (b, 0, h + off))
    pre = pl.BlockSpec((ts, HEAD_DIM), lambda b, h, i: (0, h))
    return pl.pallas_call(
        functools.partial(_sb_kernel, ts=ts, n_sub=n_sub),
        grid=(bsz, HEADS, length // tq),
        in_specs=[pl.BlockSpec((None, tq, HEAD_DIM), lambda b, h, i: (b, i, h + HEADS)),
                  seq(2 * HEADS), seq(3 * HEADS), pre, pre,
                  pl.BlockSpec((2 * ts, ts), lambda b, h, i: (0, 0))],
        out_specs=pl.BlockSpec((None, tq, HEAD_DIM), lambda b, h, i: (b, i, h)),
        out_shape=jax.ShapeDtypeStruct((bsz, length, MIX_W), BF16),
        scratch_shapes=[pltpu.VMEM((n_sub, ts, HEAD_DIM), F32), pltpu.VMEM((n_sub, ts, 1), F32)],
        compiler_params=pltpu.CompilerParams(
            dimension_semantics=("parallel", "parallel", "arbitrary"), vmem_limit_bytes=VMEM_LIMIT),
        name="stick_breaking",
    )(plain, plain, plain, kp, vp, tri)


def _merge_ln_kernel(h_ref, ohg_ref, osb_ref, ghg_ref, gsb_ref, php_ref, psb_ref, wo_ref, g_ref, b_ref, o_ref):
    y = (ghg_ref[...].astype(F32) * _dot(ohg_ref[...], php_ref[...])
         + gsb_ref[...].astype(F32) * _dot(osb_ref[...], psb_ref[...])).astype(BF16)
    tm, d = o_ref.shape
    s1 = jnp.zeros((tm, 1), F32)
    s2 = jnp.zeros((tm, 1), F32)
    for c in range(d // LN_CHUNK):
        cols = slice(c * LN_CHUNK, (c + 1) * LN_CHUNK)
        z = DN_ALPHA * h_ref[:, cols] + _dot(y, wo_ref[:, cols])
        o_ref[:, cols] = z
        s1 = s1 + jnp.sum(z, axis=-1, keepdims=True)
        s2 = s2 + jnp.sum(z * z, axis=-1, keepdims=True)
    mean = s1 * (1.0 / d)
    var = s2 * (1.0 / d) - mean * mean
    o_ref[...] = (o_ref[...] - mean) * lax.rsqrt(var + LN_EPS) * g_ref[...] + b_ref[...]


def _merge_ln(h, o_hg, o_sb, gates, p_hg, p_sb, w_out, g, b, *, tm):
    n, d = h.shape
    assert n % tm == 0
    row = lambda w: pl.BlockSpec((tm, w), lambda i: (i, 0))
    full = lambda a: pl.BlockSpec(a.shape, lambda i: (0, 0), pipeline_mode=pl.Buffered(1))
    return pl.pallas_call(
        _merge_ln_kernel,
        grid=(n // tm,),
        in_specs=[row(d), row(MIX_W), row(MIX_W),
                  pl.BlockSpec((tm, d), lambda i: (i, 0)), pl.BlockSpec((tm, d), lambda i: (i, 1)),
                  full(p_hg), full(p_sb), full(w_out), full(g), full(b)],
        out_specs=row(d),
        out_shape=jax.ShapeDtypeStruct((n, d), F32),
        compiler_params=pltpu.CompilerParams(
            dimension_semantics=("parallel",), vmem_limit_bytes=VMEM_LIMIT),
        name="merge_ln",
    )(h, o_hg, o_sb, gates, gates, p_hg, p_sb, w_out, g, b)


CAST_BLOCK_BYTES = 8 * 1024 * 1024


def _cast_kernel(w_ref, *rest):
    o_ref = rest[-1]
    w = w_ref[...]
    if len(rest) == 2:
        w = w * rest[0][...]
    o_ref[...] = w.astype(BF16)


def _to_bf16(w, col_scale=None):
    _, r, c = w.shape
    tr = 16
    while tr * 2 * c * 4 <= CAST_BLOCK_BYTES and r % (tr * 2) == 0:
        tr *= 2
    assert r % tr == 0
    in_specs = [pl.BlockSpec((None, tr, c), lambda i: (0, i, 0))]
    args = [w]
    if col_scale is not None:
        in_specs.append(pl.BlockSpec((1, c), lambda i: (0, 0)))
        args.append(col_scale)
    return pl.pallas_call(
        _cast_kernel,
        grid=(r // tr,),
        in_specs=in_specs,
        out_specs=pl.BlockSpec((tr, c), lambda i: (i, 0)),
        out_shape=jax.ShapeDtypeStruct((r, c), BF16),
        compiler_params=pltpu.CompilerParams(
            dimension_semantics=("parallel",), vmem_limit_bytes=VMEM_LIMIT),
        name="to_bf16",
    )(*args)


def _pick(n, pref):
    t = min(n, pref)
    while n % t:
        t //= 2
    return t


def kernel(x, meta, ln1_g, ln1_b, ffn1_w_gate, ffn1_w_up, ffn1_w_down, w_in, b_gate, hg_lb_logits, hg_norm_g, w_proj_hg, w_proj_sb, w_out, ln2_g, ln2_b, ffn2_w_gate, ffn2_w_up, ffn2_w_down, ln3_g, ln3_b):
    bsz, seq, d = x.shape
    assert ln1_g.shape[0] == DEPTH and hg_lb_logits.shape == (DEPTH + 1, MIX_W) and (2 * d) % MIX_W == 0
    n = bsz * seq
    tm = _pick(n, 1024)
    tf = _pick(ffn1_w_gate.shape[2], 512)
    bf = _to_bf16
    row = lambda a: a[0].reshape(1, -1)

    w1 = (bf(ffn1_w_gate), bf(ffn1_w_up), bf(ffn1_w_down), row(ln1_g), row(ln1_b))
    w2 = (bf(ffn2_w_gate), bf(ffn2_w_up), bf(ffn2_w_down), row(ln3_g), row(ln3_b))
    col = lax.broadcasted_iota(jnp.int32, (1, w_in.shape[2]), 1)
    is_sq = jnp.logical_and(col >= 4 * MIX_W, col < 5 * MIX_W)
    w_in_b = _to_bf16(w_in, jnp.where(is_sq, 1.0 / math.sqrt(HEAD_DIM), 1.0).astype(F32))
    bg = row(b_gate)
    gain = row(hg_norm_g)

    hp = jnp.concatenate([jnp.zeros((PAD, d), x.dtype), meta.astype(x.dtype)], axis=0)
    _, hpb = _ffn_ln(hp, *w1, tm=BLOCK, tf=tf, emit_bf16=True)
    _, lf_p, plain_p, _ = _proj_all(hpb, w_in_b, bg, hg_lb_logits, tm=BLOCK, n_inert=PAD)
    _, st0 = _hgrn(jnp.zeros((1, BLOCK, 2 * MIX_W), BF16), lf_p[None], plain_p[None], gain,
                   jnp.zeros((HEADS, HEAD_DIM, HEAD_DIM), F32), tt=BLOCK, hpb=HG_HEADS_PER_STEP)
    pad_rows = ((0, SB_TILE - BLOCK), (0, 0))
    kp = jnp.pad(plain_p[:, 2 * MIX_W:3 * MIX_W], pad_rows)
    vp = jnp.pad(plain_p[:, 3 * MIX_W:], pad_rows)

    h, hb = _ffn_ln(x.reshape(n, d), *w1, tm=tm, tf=tf, emit_bf16=True)
    q_og, lf, plain, gates = _proj_all(hb, w_in_b, bg, hg_lb_logits, tm=_pick(n, 2048), n_inert=0)
    seq3 = lambda a: a.reshape(bsz, seq, a.shape[-1])
    o_hg, _ = _hgrn(seq3(q_og), seq3(lf), seq3(plain), gain, st0[0], tt=_pick(seq, 256),
                    hpb=HG_HEADS_PER_STEP)
    o_sb = _stick_breaking(seq3(plain), kp, vp, n_sub=_pick(seq // SB_TILE, 8))
    h = _merge_ln(h, o_hg.reshape(n, MIX_W), o_sb.reshape(n, MIX_W), gates, bf(w_proj_hg), bf(w_proj_sb),
                  bf(w_out), row(ln2_g), row(ln2_b), tm=_pick(n, 512))
    (h,) = _ffn_ln(h, *w2, tm=tm, tf=tf, emit_bf16=False)
    return h.reshape(bsz, seq, d)
```

```python
import functools
import math

import jax
import jax.numpy as jnp
from jax import lax
from jax.experimental import pallas as pl
from jax.experimental.pallas import tpu as pltpu

N_META = 16
BLOCK = 128
PAD = BLOCK - N_META
HEADS = 8
HEAD_DIM = 128
MIX_W = HEADS * HEAD_DIM
SUB = 8
HALO = 8
HG_HEADS_PER_STEP = 8
HG_LEVEL_TILE = 128
LOG2_E = math.log2(math.e)
SB_TILE = 256
LN_EPS = 1e-5
RMS_EPS = 1e-6
DEPTH = 1
DN_ALPHA = (2.0 * DEPTH) ** 0.25
SB_DEAD_LOG = -104.0
LN_CHUNK = 512
MXU_COLS = 256
VMEM_LIMIT = 60 * 1024 * 1024

F32 = jnp.float32
BF16 = jnp.bfloat16


def _layer_norm(y, g, b):
    mu = jnp.mean(y, axis=-1, keepdims=True)
    yc = y - mu
    var = jnp.mean(yc * yc, axis=-1, keepdims=True)
    return yc * lax.rsqrt(var + LN_EPS) * g + b


def _dot(a, b):
    return jnp.dot(a, b, preferred_element_type=F32)


def _dot_nt(a, b):
    return lax.dot_general(a, b, (((1,), (1,)), ((), ())), preferred_element_type=F32)


def _dot_tn(a, b):
    return lax.dot_general(a, b, (((0,), (0,)), ((), ())), preferred_element_type=F32)


def _split_bf16(x):
    hi = x.astype(BF16)
    return hi, (x - hi.astype(F32)).astype(BF16)


def _ffn_ln_kernel(x_hbm, wg_ref, wu_ref, wd_ref, g_ref, b_ref, o_ref, *rest, emit_bf16, tm, n_j):
    xb_ref, xbuf_ref, sem = rest
    ob_ref = xb_ref if emit_bf16 else None
    i = pl.program_id(0)
    j = pl.program_id(1)
    n_i = pl.num_programs(0)

    def x_copy(tile):
        return pltpu.make_async_copy(x_hbm.at[pl.ds(pl.multiple_of(tile * tm, tm), tm), :], xbuf_ref, sem)

    @pl.when(jnp.logical_and(i == 0, j == 0))
    def _():
        x_copy(0).start()

    def swiglu_hidden():
        xb = xb_ref[...]
        gate = _dot(xb, wg_ref[...])
        up = _dot(xb, wu_ref[...])
        return (gate * jax.nn.sigmoid(gate) * up).astype(BF16)

    @pl.when(j == 0)
    def _():
        x_copy(i).wait()
        xb_ref[...] = xbuf_ref[...].astype(BF16)
        if n_j == 1:
            o_ref[...] = (DN_ALPHA / 0.5) * xbuf_ref[...]

    if n_j > 1:
        @pl.when(j == 0)
        def _():
            o_ref[...] = (DN_ALPHA / 0.5) * xbuf_ref[...] + _dot(swiglu_hidden(), wd_ref[...])

    @pl.when(jnp.logical_and(j == min(1, n_j - 1), i + 1 < n_i))
    def _():
        x_copy(i + 1).start()

    @pl.when(jnp.logical_and(j > 0, j < n_j - 1))
    def _():
        o_ref[...] += _dot(swiglu_hidden(), wd_ref[...])

    @pl.when(j == n_j - 1)
    def _():
        act = swiglu_hidden()
        d = o_ref.shape[1]
        s1 = jnp.zeros((tm, 1), F32)
        s2 = jnp.zeros((tm, 1), F32)
        for c in range(d // LN_CHUNK):
            cols = slice(c * LN_CHUNK, (c + 1) * LN_CHUNK)
            z = o_ref[:, cols] + _dot(act, wd_ref[:, cols])
            o_ref[:, cols] = z
            s1 = s1 + jnp.sum(z, axis=-1, keepdims=True)
            s2 = s2 + jnp.sum(z * z, axis=-1, keepdims=True)
        mean = s1 * (1.0 / d)
        var = s2 * (1.0 / d) - mean * mean
        y = (o_ref[...] - mean) * lax.rsqrt(var + 4.0 * LN_EPS) * g_ref[...] + b_ref[...]
        o_ref[...] = y
        if ob_ref is not None:
            ob_ref[...] = y.astype(BF16)


def _ffn_ln(x, wg, wu, wd, g, b, *, tm, tf, emit_bf16):
    n, d = x.shape
    dff = wg.shape[1]
    assert n % tm == 0 and dff % tf == 0
    rows = pl.BlockSpec((tm, d), lambda i, j: (i, 0))
    out_shape = [jax.ShapeDtypeStruct((n, d), F32)]
    if emit_bf16:
        out_shape.append(jax.ShapeDtypeStruct((n, d), BF16))
    return pl.pallas_call(
        functools.partial(_ffn_ln_kernel, emit_bf16=emit_bf16, tm=tm, n_j=dff // tf),
        grid=(n // tm, dff // tf),
        in_specs=[
            pl.BlockSpec(memory_space=pl.ANY),
            pl.BlockSpec((d, tf), lambda i, j: (0, j)),
            pl.BlockSpec((d, tf), lambda i, j: (0, j)),
            pl.BlockSpec((tf, d), lambda i, j: (j, 0)),
            pl.BlockSpec((1, d), lambda i, j: (0, 0)),
            pl.BlockSpec((1, d), lambda i, j: (0, 0)),
        ],
        out_specs=[rows] * len(out_shape),
        out_shape=out_shape,
        scratch_shapes=([] if emit_bf16 else [pltpu.VMEM((tm, d), BF16)])
        + [pltpu.VMEM((tm, d), F32), pltpu.SemaphoreType.DMA(())],
        compiler_params=pltpu.CompilerParams(
            dimension_semantics=("arbitrary", "arbitrary"), vmem_limit_bytes=VMEM_LIMIT),
        name="ffn_ln",
    )(x, wg, wu, wd, g, b)


def _proj_kernel(x_ref, w_ref, *rest, act, n_inert, tm):
    o_ref = rest[-1]
    x = x_ref[...]
    for c in range(MIX_W // MXU_COLS):
        cols = slice(c * MXU_COLS, (c + 1) * MXU_COLS)
        acc = _dot(x, w_ref[:, cols])
        if act == "silu":
            y = acc * jax.nn.sigmoid(acc)
        elif act == "gate":
            y = jax.nn.sigmoid(acc + rest[0][:, cols])
        elif act == "log_forget":
            l0 = rest[0][0:1, cols]
            l1 = rest[0][1:2, cols]
            m = jnp.maximum(l0, l1)
            e0 = jnp.exp(l0 - m)
            e1 = jnp.exp(l1 - m)
            lb = e0 / (e0 + e1)
            y = jnp.log(lb + (1.0 - lb) * jax.nn.sigmoid(acc))
            if n_inert:
                row = pl.program_id(0) * tm + lax.broadcasted_iota(jnp.int32, y.shape, 0)
                y = jnp.where(row >= n_inert, y, 0.0)
        else:
            y = acc
        o_ref[:, cols] = y.astype(o_ref.dtype)


def _proj(xb, w_in, extra, *, act, col_block, n_blocks, out_dtype, tm, n_inert=0):
    n, d = xb.shape
    assert n % tm == 0
    in_specs = [pl.BlockSpec((tm, d), lambda i, j: (i, 0)),
                pl.BlockSpec((d, MIX_W), lambda i, j: (0, col_block(j)))]
    args = [xb, w_in]
    if act == "gate":
        in_specs.append(pl.BlockSpec((1, MIX_W), lambda i, j: (0, j)))
        args.append(extra)
    elif act == "log_forget":
        in_specs.append(pl.BlockSpec(extra.shape, lambda i, j: (0, 0)))
        args.append(extra)
    return pl.pallas_call(
        functools.partial(_proj_kernel, act=act, n_inert=n_inert, tm=tm),
        grid=(n // tm, n_blocks),
        in_specs=in_specs,
        out_specs=pl.BlockSpec((tm, MIX_W), lambda i, j: (i, j)),
        out_shape=jax.ShapeDtypeStruct((n, n_blocks * MIX_W), out_dtype),
        compiler_params=pltpu.CompilerParams(
            dimension_semantics=("parallel", "arbitrary"), vmem_limit_bytes=VMEM_LIMIT),
        name="proj_" + act,
    )(*args)


def _proj_all(xb, w_in, b_gate, lb_logits, *, tm, n_inert):
    d = xb.shape[1]
    silu = _proj(xb, w_in, None, act="silu", col_block=lambda j: 3 * j, n_blocks=2, out_dtype=BF16, tm=tm)
    log_f = _proj(xb, w_in, lb_logits, act="log_forget", col_block=lambda j: 1, n_blocks=1, out_dtype=F32,
                  tm=tm, n_inert=n_inert)
    plain = _proj(xb, w_in, None, act="none", col_block=lambda j: jnp.where(j > 0, j + 3, 2), n_blocks=4,
                  out_dtype=BF16, tm=tm)
    gates = _proj(xb, w_in, b_gate, act="gate", col_block=lambda j: j + 7, n_blocks=2 * d // MIX_W,
                  out_dtype=BF16, tm=tm)
    return silu, log_f, plain, gates


def _hgrn_levels(tt):
    return [m for m in (2 * SUB, 4 * SUB, 8 * SUB, 16 * SUB, 32 * SUB, 64 * SUB, 128 * SUB) if m <= tt]


def _hgrn_kernel(q_ref, lf_ref, v_ref, og_ref, gain_ref, st0_ref, tri_ref, lvl_ref, o_ref, stf_ref, st_ref,
                 fpad_ref, kpad_ref, vpad_ref, *, tt, lt, hpb):
    t_id = pl.program_id(2)

    @pl.when(t_id == 0)
    def _():
        st_ref[...] = st0_ref[...]

    row = lax.broadcasted_iota(jnp.int32, (tt, HEAD_DIM), 0)
    sub_start = (row & (SUB - 1)) == 0
    zeros = jnp.zeros((HALO, HEAD_DIM), F32)
    lvl = lvl_ref[...]

    for hd in range(hpb):
        lanes = slice(hd * HEAD_DIM, (hd + 1) * HEAD_DIM)
        qb = q_ref[:, lanes]
        q = qb.astype(F32)
        g2 = lf_ref[:, lanes] * LOG2_E
        vb = v_ref[:, lanes]
        v = vb.astype(F32)
        f = jnp.exp2(g2)
        kk = 1.0 - f
        kb = kk.astype(BF16)
        g_hi, g_lo = _split_bf16(g2)

        fpad_ref[hd, 0:HALO, :] = zeros
        kpad_ref[hd, 0:HALO, :] = zeros
        vpad_ref[hd, 0:HALO, :] = zeros
        fpad_ref[hd, HALO:, :] = jnp.where(sub_start, 0.0, f)
        kpad_ref[hd, HALO:, :] = kk
        vpad_ref[hd, HALO:, :] = v
        o = jnp.sum(q * kk, axis=-1, keepdims=True) * v
        u = q
        for d in range(1, SUB):
            u = u * fpad_ref[hd, pl.ds(HALO - d + 1, tt), :]
            k_d = kpad_ref[hd, pl.ds(HALO - d, tt), :]
            v_d = vpad_ref[hd, pl.ds(HALO - d, tt), :]
            o = o + jnp.sum(u * k_d, axis=-1, keepdims=True) * v_d

        for s in range(tt // lt):
            rows = slice(s * lt, (s + 1) * lt)
            b = _dot(tri_ref[...], jnp.concatenate([g_hi[rows], g_lo[rows]], axis=0))

            scores = jnp.zeros((lt, lt), F32)
            for idx, m in enumerate(_hgrn_levels(lt)):
                mid = b.reshape(lt // m, m, HEAD_DIM)[:, m // 2 - 1:m // 2, :]
                b_mid = jnp.broadcast_to(mid, (lt // m, m, HEAD_DIM)).reshape(lt, HEAD_DIM)
                e = jnp.exp2(-jnp.abs(b - b_mid)).astype(BF16)
                scores = jnp.where(lvl == idx, _dot_nt(qb[rows] * e, kb[rows] * e), scores)
            st = st_ref[hd]
            o_s = (o[rows] + _dot(scores.astype(BF16), vb[rows])
                   + _dot_nt(qb[rows] * jnp.exp2(b).astype(BF16), st.astype(BF16)))

            b_end = b[lt - 1:lt, :]
            k_end = kb[rows] * jnp.exp2(b_end - b).astype(BF16)
            st_ref[hd] = st * jnp.exp2(b_end) + _dot_tn(vb[rows], k_end)

            o_s = o_s * lax.rsqrt(jnp.mean(o_s * o_s, axis=-1, keepdims=True) + RMS_EPS)
            o_ref[rows, lanes] = (o_s * gain_ref[:, lanes] * og_ref[rows, lanes].astype(F32)).astype(BF16)

    @pl.when(t_id == pl.num_programs(2) - 1)
    def _():
        stf_ref[...] = st_ref[...]


def _hgrn(q_og, lf, plain, gain, st0, *, tt, hpb):
    bsz, length, _ = lf.shape
    lt = min(tt, HG_LEVEL_TILE)
    levels = _hgrn_levels(lt)
    assert length % tt == 0 and tt % lt == 0 and levels[-1] == lt and SUB * 2 == levels[0] and HEADS % hpb == 0
    r = lax.broadcasted_iota(jnp.int32, (lt, 2 * lt), 0)
    c = lax.broadcasted_iota(jnp.int32, (lt, 2 * lt), 1) % lt
    tri = (c <= r).astype(BF16)
    r = lax.broadcasted_iota(jnp.int32, (lt, lt), 0)
    c = lax.broadcasted_iota(jnp.int32, (lt, lt), 1)
    lvl = jnp.full((lt, lt), -1, jnp.int32)
    for idx, m in reversed(list(enumerate(levels))):
        lvl = jnp.where(r // m == c // m, idx, lvl)
    lvl = jnp.where(jnp.logical_or(r // SUB == c // SUB, c > r), -1, lvl)
    groups = HEADS // hpb
    width = hpb * HEAD_DIM
    tok = lambda off: pl.BlockSpec((None, tt, width), lambda b, h, t: (b, t, h + off))
    const = lambda a: pl.BlockSpec(a.shape, lambda b, h, t: (0, 0))
    return pl.pallas_call(
        functools.partial(_hgrn_kernel, tt=tt, lt=lt, hpb=hpb),
        grid=(bsz, groups, length // tt),
        in_specs=[tok(0), tok(0), tok(0), tok(groups),
                  pl.BlockSpec((1, width), lambda b, h, t: (0, h)),
                  pl.BlockSpec((hpb, HEAD_DIM, HEAD_DIM), lambda b, h, t: (h, 0, 0)),
                  const(tri), const(lvl)],
        out_specs=[tok(0), pl.BlockSpec((None, hpb, HEAD_DIM, HEAD_DIM), lambda b, h, t: (b, h, 0, 0))],
        out_shape=[jax.ShapeDtypeStruct((bsz, length, MIX_W), BF16),
                   jax.ShapeDtypeStruct((bsz, HEADS, HEAD_DIM, HEAD_DIM), F32)],
        scratch_shapes=[pltpu.VMEM((hpb, HEAD_DIM, HEAD_DIM), F32)]
        + [pltpu.VMEM((hpb, tt + HALO, HEAD_DIM), F32)] * 3,
        compiler_params=pltpu.CompilerParams(
            dimension_semantics=("parallel", "parallel", "arbitrary"), vmem_limit_bytes=VMEM_LIMIT),
        name="hgrn2",
    )(q_og, lf, plain, q_og, gain, st0, tri, lvl)


def _sb_kernel(q_ref, k_ref, v_ref, kp_ref, vp_ref, tri_ref, o_ref, acc_ref, rem_ref, *, ts, n_sub):
    qi = pl.program_id(2)
    r_i = lax.broadcasted_iota(jnp.int32, (ts, ts), 0)
    c_i = lax.broadcasted_iota(jnp.int32, (ts, ts), 1)
    causal = c_i < r_i
    prefix_valid = jnp.logical_and(c_i >= PAD, c_i < BLOCK)

    def tile(j):
        start = pl.multiple_of(j * ts, ts)
        return k_ref[pl.ds(start, ts), :], v_ref[pl.ds(start, ts), :]

    def log_weights(q, k_t, mask):
        z = _dot_nt(q, k_t)
        log_beta = jnp.minimum(z, 0.0) - jnp.log(1.0 + jnp.exp(-jnp.abs(z)))
        log_rest = log_beta - z
        if mask is not None:
            log_rest = jnp.where(mask, log_rest, 0.0)
        hi, lo = _split_bf16(log_rest)
        later = _dot(jnp.concatenate([hi, lo], axis=1), tri_ref[...])
        return log_beta + later, jnp.sum(log_rest, axis=-1, keepdims=True)

    def weights(log_w, mask):
        w = jnp.exp(log_w)
        if mask is not None:
            w = jnp.where(mask, w, 0.0)
        return w.astype(BF16)

    def first_visits(sub0_has_prev):
        for s in range(n_sub):
            q = q_ref[s * ts:(s + 1) * ts, :]
            g = qi * n_sub + s
            k_t, v_t = tile(g)
            log_w, rem = log_weights(q, k_t, causal)
            acc = _dot(weights(log_w, causal), v_t)
            if s > 0 or sub0_has_prev:
                k_t, v_t = tile(g - 1)
                log_w, total = log_weights(q, k_t, None)
                acc = acc + _dot(weights(log_w + rem, None), v_t)
                rem = rem + total
            acc_ref[s] = acc
            rem_ref[s] = rem

    @pl.when(qi > 0)
    def _():
        first_visits(True)

    @pl.when(qi == 0)
    def _():
        first_visits(False)

    def finish(s):
        q = q_ref[s * ts:(s + 1) * ts, :]

        def visit(k_t, v_t, mask):
            log_w, total = log_weights(q, k_t, mask)
            rem = rem_ref[s]
            acc_ref[s] += _dot(weights(log_w + rem, mask), v_t)
            rem = rem + total
            rem_ref[s] = rem
            return jnp.max(rem)

        def cond(state):
            j, live = state
            return jnp.logical_and(j >= 0, live > SB_DEAD_LOG)

        def body(state):
            j, _ = state
            return j - 1, visit(*tile(j), None)

        _, live = lax.while_loop(cond, body, (qi * n_sub + s - 2, jnp.max(rem_ref[s])))

        @pl.when(live > SB_DEAD_LOG)
        def _():
            visit(kp_ref[...], vp_ref[...], prefix_valid)

    @pl.when(jnp.max(rem_ref[...]) > SB_DEAD_LOG)
    def _():
        for s in range(n_sub):
            finish(s)

    for s in range(n_sub):
        o_ref[s * ts:(s + 1) * ts, :] = acc_ref[s].astype(BF16)


def _stick_breaking(plain, kp, vp, *, n_sub):
    bsz, length, _ = plain.shape
    ts = SB_TILE
    tq = ts * n_sub
    assert length % tq == 0
    r = lax.broadcasted_iota(jnp.int32, (2 * ts, ts), 0) % ts
    c = lax.broadcasted_iota(jnp.int32, (2 * ts, ts), 1)
    tri = (r > c).astype(BF16)
    seq = lambda off: pl.BlockSpec((None, length, HEAD_DIM), lambda b, h, i: (b, 0, h + off))
    pre = pl.BlockSpec((ts, HEAD_DIM), lambda b, h, i: (0, h))
    return pl.pallas_call(
        functools.partial(_sb_kernel, ts=ts, n_sub=n_sub),
        grid=(bsz, HEADS, length // tq),
        in_specs=[pl.BlockSpec((None, tq, HEAD_DIM), lambda b, h, i: (b, i, h + HEADS)),
                  seq(2 * HEADS), seq(3 * HEADS), pre, pre,
                  pl.BlockSpec((2 * ts, ts), lambda b, h, i: (0, 0))],
        out_specs=pl.BlockSpec((None, tq, HEAD_DIM), lambda b, h, i: (b, i, h)),
        out_shape=jax.ShapeDtypeStruct((bsz, length, MIX_W), BF16),
        scratch_shapes=[pltpu.VMEM((n_sub, ts, HEAD_DIM), F32), pltpu.VMEM((n_sub, ts, 1), F32)],
        compiler_params=pltpu.CompilerParams(
            dimension_semantics=("parallel", "parallel", "arbitrary"), vmem_limit_bytes=VMEM_LIMIT),
        name="stick_breaking",
    )(plain, plain, plain, kp, vp, tri)


def _merge_ln_kernel(h_ref, ohg_ref, osb_ref, ghg_ref, gsb_ref, php_ref, psb_ref, wo_ref, g_ref, b_ref, o_ref):
    y = (ghg_ref[...].astype(F32) * _dot(ohg_ref[...], php_ref[...])
         + gsb_ref[...].astype(F32) * _dot(osb_ref[...], psb_ref[...])).astype(BF16)
    tm, d = o_ref.shape
    s1 = jnp.zeros((tm, 1), F32)
    s2 = jnp.zeros((tm, 1), F32)
    for c in range(d // LN_CHUNK):
        cols = slice(c * LN_CHUNK, (c + 1) * LN_CHUNK)
        z = DN_ALPHA * h_ref[:, cols] + _dot(y, wo_ref[:, cols])
        o_ref[:, cols] = z
        s1 = s1 + jnp.sum(z, axis=-1, keepdims=True)
        s2 = s2 + jnp.sum(z * z, axis=-1, keepdims=True)
    mean = s1 * (1.0 / d)
    var = s2 * (1.0 / d) - mean * mean
    o_ref[...] = (o_ref[...] - mean) * lax.rsqrt(var + LN_EPS) * g_ref[...] + b_ref[...]


def _merge_ln(h, o_hg, o_sb, gates, p_hg, p_sb, w_out, g, b, *, tm):
    n, d = h.shape
    assert n % tm == 0
    row = lambda w: pl.BlockSpec((tm, w), lambda i: (i, 0))
    full = lambda a: pl.BlockSpec(a.shape, lambda i: (0, 0), pipeline_mode=pl.Buffered(1))
    return pl.pallas_call(
        _merge_ln_kernel,
        grid=(n // tm,),
        in_specs=[row(d), row(MIX_W), row(MIX_W),
                  pl.BlockSpec((tm, d), lambda i: (i, 0)), pl.BlockSpec((tm, d), lambda i: (i, 1)),
                  full(p_hg), full(p_sb), full(w_out), full(g), full(b)],
        out_specs=row(d),
        out_shape=jax.ShapeDtypeStruct((n, d), F32),
        compiler_params=pltpu.CompilerParams(
            dimension_semantics=("parallel",), vmem_limit_bytes=VMEM_LIMIT),
        name="merge_ln",
    )(h, o_hg, o_sb, gates, gates, p_hg, p_sb, w_out, g, b)


CAST_BLOCK_BYTES = 8 * 1024 * 1024


def _cast_kernel(w_ref, *rest):
    o_ref = rest[-1]
    w = w_ref[...]
    if len(rest) == 2:
        w = w * rest[0][...]
    o_ref[...] = w.astype(BF16)


def _to_bf16(w, col_scale=None):
    _, r, c = w.shape
    tr = 16
    while tr * 2 * c * 4 <= CAST_BLOCK_BYTES and r % (tr * 2) == 0:
        tr *= 2
    assert r % tr == 0
    in_specs = [pl.BlockSpec((None, tr, c), lambda i: (0, i, 0))]
    args = [w]
    if col_scale is not None:
        in_specs.append(pl.BlockSpec((1, c), lambda i: (0, 0)))
        args.append(col_scale)
    return pl.pallas_call(
        _cast_kernel,
        grid=(r // tr,),
        in_specs=in_specs,
        out_specs=pl.BlockSpec((tr, c), lambda i: (i, 0)),
        out_shape=jax.ShapeDtypeStruct((r, c), BF16),
        compiler_params=pltpu.CompilerParams(
            dimension_semantics=("parallel",), vmem_limit_bytes=VMEM_LIMIT),
        name="to_bf16",
    )(*args)


def _pick(n, pref):
    t = min(n, pref)
    while n % t:
        t //= 2
    return t


def kernel(x, meta, ln1_g, ln1_b, ffn1_w_gate, ffn1_w_up, ffn1_w_down, w_in, b_gate, hg_lb_logits, hg_norm_g, w_proj_hg, w_proj_sb, w_out, ln2_g, ln2_b, ffn2_w_gate, ffn2_w_up, ffn2_w_down, ln3_g, ln3_b):
    bsz, seq, d = x.shape
    assert ln1_g.shape[0] == DEPTH and hg_lb_logits.shape == (DEPTH + 1, MIX_W) and (2 * d) % MIX_W == 0
    n = bsz * seq
    tm = _pick(n, 1024)
    tf = _pick(ffn1_w_gate.shape[2], 512)
    bf = _to_bf16
    row = lambda a: a[0].reshape(1, -1)

    w1 = (bf(ffn1_w_gate), bf(ffn1_w_up), bf(ffn1_w_down), row(ln1_g), row(ln1_b))
    w2 = (bf(ffn2_w_gate), bf(ffn2_w_up), bf(ffn2_w_down), row(ln3_g), row(ln3_b))
    col = lax.broadcasted_iota(jnp.int32, (1, w_in.shape[2]), 1)
    is_sq = jnp.logical_and(col >= 4 * MIX_W, col < 5 * MIX_W)
    w_in_b = _to_bf16(w_in, jnp.where(is_sq, 1.0 / math.sqrt(HEAD_DIM), 1.0).astype(F32))
    bg = row(b_gate)
    gain = row(hg_norm_g)

    hp = jnp.concatenate([jnp.zeros((PAD, d), x.dtype), meta.astype(x.dtype)], axis=0)
    _, hpb = _ffn_ln(hp, *w1, tm=BLOCK, tf=tf, emit_bf16=True)
    _, lf_p, plain_p, _ = _proj_all(hpb, w_in_b, bg, hg_lb_logits, tm=BLOCK, n_inert=PAD)
    _, st0 = _hgrn(jnp.zeros((1, BLOCK, 2 * MIX_W), BF16), lf_p[None], plain_p[None], gain,
                   jnp.zeros((HEADS, HEAD_DIM, HEAD_DIM), F32), tt=BLOCK, hpb=HG_HEADS_PER_STEP)
    pad_rows = ((0, SB_TILE - BLOCK), (0, 0))
    kp = jnp.pad(plain_p[:, 2 * MIX_W:3 * MIX_W], pad_rows)
    vp = jnp.pad(plain_p[:, 3 * MIX_W:], pad_rows)

    h, hb = _ffn_ln(x.reshape(n, d), *w1, tm=tm, tf=tf, emit_bf16=True)
    q_og, lf, plain, gates = _proj_all(hb, w_in_b, bg, hg_lb_logits, tm=_pick(n, 2048), n_inert=0)
    seq3 = lambda a: a.reshape(bsz, seq, a.shape[-1])
    o_hg, _ = _hgrn(seq3(q_og), seq3(lf), seq3(plain), gain, st0[0], tt=_pick(seq, 256),
                    hpb=HG_HEADS_PER_STEP)
    o_sb = _stick_breaking(seq3(plain), kp, vp, n_sub=_pick(seq // SB_TILE, 16))
    h = _merge_ln(h, o_hg.reshape(n, MIX_W), o_sb.reshape(n, MIX_W), gates, bf(w_proj_hg), bf(w_proj_sb),
                  bf(w_out), row(ln2_g), row(ln2_b), tm=_pick(n, 512))
    (h,) = _ffn_ln(h, *w2, tm=tm, tf=tf, emit_bf16=False)
    return h.reshape(bsz, seq, d)
```
